```python
import math
import jax, jax.numpy as jnp
from jax import lax
import numpy as np


D_MODEL = 1024
BATCH = 4
SEQ = 8192
DEPTH = 1

CHUNK = 64
RWKV_HEADS = 16
RWKV_HEAD_DIM = 64
RWKV_WIDTH = RWKV_HEADS * RWKV_HEAD_DIM
DECAY_LORA = 64
ICLR_LORA = 64
S5_GROUPS = 32
S5_GROUP_DIM = 16
S5_WIDTH = S5_GROUPS * S5_GROUP_DIM
S5_STATE = 64
SHIFT_WIDTH = 3 * RWKV_WIDTH + DECAY_LORA + ICLR_LORA
IN_WIDTH = SHIFT_WIDTH + RWKV_WIDTH + 2 * S5_WIDTH + 2 * D_MODEL
SPLIT_POINTS = (SHIFT_WIDTH,
                SHIFT_WIDTH + RWKV_WIDTH,
                SHIFT_WIDTH + RWKV_WIDTH + S5_WIDTH,
                SHIFT_WIDTH + RWKV_WIDTH + 2 * S5_WIDTH,
                SHIFT_WIDTH + RWKV_WIDTH + 2 * S5_WIDTH + D_MODEL)
RWKV_SPLITS = (RWKV_WIDTH, 2 * RWKV_WIDTH, 3 * RWKV_WIDTH, 3 * RWKV_WIDTH + DECAY_LORA)
RMS_EPS = 1e-6
LNX_EPS = 64e-5
DT_MIN = 1e-3
DT_MAX = 1e-1

kernel_name = 'rwkv7_s5_gated_hybrid_block'


def rms_norm(x, g):
    xf = x.astype(jnp.float32)
    y = xf * lax.rsqrt(jnp.mean(xf * xf, axis=-1, keepdims=True) + RMS_EPS)
    return (y * g.astype(jnp.float32)).astype(x.dtype)


def token_shift(z):
    return jnp.pad(z, ((0, 0), (1, 0), (0, 0)))[:, :-1]


def wkv7(r, w, k, v, a, b):
    bsz, seq, nh, nd = r.shape
    n_chunks = seq // CHUNK

    def to_chunks(t):
        return jnp.moveaxis(t, 1, 0).reshape(n_chunks, CHUNK, bsz, nh, nd)

    def frame_step(S, inp):
        r_t, w_t, k_t, v_t, a_t, b_t = inp
        sa = jnp.einsum('bhij,bhj->bhi', S, a_t)
        S = S * w_t[:, :, None, :] + sa[..., None] * b_t[:, :, None, :] + v_t[..., None] * k_t[:, :, None, :]
        return S, jnp.einsum('bhij,bhj->bhi', S, r_t)

    def chunk_step(S, inp):
        return lax.scan(frame_step, S, inp)

    S0 = jnp.zeros((bsz, nh, nd, nd), jnp.float32)
    _, y = lax.scan(chunk_step, S0, (to_chunks(r), to_chunks(w), to_chunks(k),
                                      to_chunks(v), to_chunks(a), to_chunks(b)))
    return jnp.moveaxis(y.reshape(seq, bsz, nh, nd), 0, 1)


def s5_scan(u, lam_re, lam_im, log_dt, b_re, b_im, c_re, c_im, d_skip):
    f32 = jnp.float32
    lam_re = lam_re.astype(f32)
    lam_im = lam_im.astype(f32)
    dt = jnp.exp(log_dt.astype(f32))[:, None]
    mag = jnp.exp(lam_re * dt)
    ang = lam_im * dt
    ab_re = mag * jnp.cos(ang)
    ab_im = mag * jnp.sin(ang)
    den = lam_re * lam_re + lam_im * lam_im
    nr = ab_re - 1.0
    f_re = (nr * lam_re + ab_im * lam_im) / den
    f_im = (ab_im * lam_re - nr * lam_im) / den
    b_re = b_re.astype(f32)
    b_im = b_im.astype(f32)
    bb_re = f_re[..., None] * b_re - f_im[..., None] * b_im
    bb_im = f_re[..., None] * b_im + f_im[..., None] * b_re
    u_t = jnp.moveaxis(u, 1, 0)
    bu_re = jnp.einsum('lbgh,gph->lbgp', u_t, bb_re)
    bu_im = jnp.einsum('lbgh,gph->lbgp', u_t, bb_im)
    seq = u_t.shape[0]
    a_re = jnp.broadcast_to(ab_re, (seq, 1) + ab_re.shape)
    a_im = jnp.broadcast_to(ab_im, (seq, 1) + ab_im.shape)

    def combine(e1, e2):
        a1r, a1i, b1r, b1i = e1
        a2r, a2i, b2r, b2i = e2
        return (a2r * a1r - a2i * a1i,
                a2r * a1i + a2i * a1r,
                a2r * b1r - a2i * b1i + b2r,
                a2r * b1i + a2i * b1r + b2i)

    _, _, s_re, s_im = lax.associative_scan(combine, (a_re, a_im, bu_re, bu_im), axis=0)
    y = (jnp.einsum('lbgp,ghp->blgh', s_re, c_re.astype(f32))
         - jnp.einsum('lbgp,ghp->blgh', s_im, c_im.astype(f32)))
    return y + d_skip.astype(f32) * u


def setup_inputs(seed: int = 0) -> dict:
    key = jax.random.key(seed)
    ks = jax.random.split(key, 32)
    f32 = jnp.float32
    L = DEPTH

    def nrm(k, shape, scale):
        return scale * jax.random.normal(k, shape, f32)

    x = jax.random.normal(ks[0], (BATCH, SEQ, D_MODEL), f32)
    norm_g = 1.0 + nrm(ks[1], (L, D_MODEL), 0.02)
    w_in = nrm(ks[2], (L, D_MODEL, IN_WIDTH), D_MODEL ** -0.5)
    mu_shift = jax.random.uniform(ks[3], (L, SHIFT_WIDTH), f32, 0.1, 0.9)
    ratio = jnp.arange(RWKV_WIDTH, dtype=f32) / (RWKV_WIDTH - 1)
    w0 = (-7.0 + 5.0 * ratio ** 0.85 + 0.5)[None, :] + nrm(ks[4], (L, RWKV_WIDTH), 0.1)
    w_up = nrm(ks[5], (L, DECAY_LORA, RWKV_WIDTH), 0.5 * DECAY_LORA ** -0.5)
    a0 = nrm(ks[6], (L, RWKV_WIDTH), 0.1)
    a_up = nrm(ks[7], (L, ICLR_LORA, RWKV_WIDTH), 0.5 * ICLR_LORA ** -0.5)
    k_k = 0.85 + nrm(ks[8], (L, RWKV_WIDTH), 0.02)
    k_a = 1.0 + nrm(ks[9], (L, RWKV_WIDTH), 0.02)
    r_k = nrm(ks[10], (L, RWKV_HEADS, RWKV_HEAD_DIM), 0.1)
    lnx_g = 1.0 + nrm(ks[11], (L, RWKV_WIDTH), 0.02)
    lnx_b = nrm(ks[12], (L, RWKV_WIDTH), 0.02)
    n_idx = jnp.arange(S5_STATE, dtype=f32)
    lam_re = -0.5 + nrm(ks[13], (L, S5_GROUPS, S5_STATE), 0.01)
    lam_im = math.pi * n_idx[None, None, :] + nrm(ks[14], (L, S5_GROUPS, S5_STATE), 0.01)
    log_dt = jax.random.uniform(ks[15], (L, S5_GROUPS), f32, math.log(DT_MIN), math.log(DT_MAX))
    b_re = nrm(ks[16], (L, S5_GROUPS, S5_STATE, S5_GROUP_DIM), (2 * S5_GROUP_DIM) ** -0.5)
    b_im = nrm(ks[17], (L, S5_GROUPS, S5_STATE, S5_GROUP_DIM), (2 * S5_GROUP_DIM) ** -0.5)
    c_re = nrm(ks[18], (L, S5_GROUPS, S5_GROUP_DIM, S5_STATE), S5_STATE ** -0.5)
    c_im = nrm(ks[19], (L, S5_GROUPS, S5_GROUP_DIM, S5_STATE), S5_STATE ** -0.5)
    d_skip = nrm(ks[20], (L, S5_GROUPS, S5_GROUP_DIM), 0.5)
    w_glu = nrm(ks[21], (L, S5_WIDTH, 2 * S5_WIDTH), S5_WIDTH ** -0.5)
    b_glu = nrm(ks[22], (L, 2 * S5_WIDTH), 0.02)
    p_a = nrm(ks[23], (L, RWKV_WIDTH, D_MODEL), RWKV_WIDTH ** -0.5)
    p_b = nrm(ks[24], (L, S5_WIDTH, D_MODEL), S5_WIDTH ** -0.5)
    w_out = nrm(ks[25], (L, D_MODEL, D_MODEL), D_MODEL ** -0.5)
    final_g = 1.0 + nrm(ks[26], (D_MODEL,), 0.02)
    return {'x': x, 'norm_g': norm_g, 'w_in': w_in, 'mu_shift': mu_shift, 'w0': w0,
            'w_up': w_up, 'a0': a0, 'a_up': a_up, 'k_k': k_k, 'k_a': k_a, 'r_k': r_k,
            'lnx_g': lnx_g, 'lnx_b': lnx_b, 'lam_re': lam_re, 'lam_im': lam_im,
            'log_dt': log_dt, 'b_re': b_re, 'b_im': b_im, 'c_re': c_re, 'c_im': c_im,
            'd_skip': d_skip, 'w_glu': w_glu, 'b_glu': b_glu, 'p_a': p_a, 'p_b': p_b,
            'w_out': w_out, 'final_g': final_g}


def reference(x, norm_g, w_in, mu_shift, w0, w_up, a0, a_up, k_k, k_a, r_k, lnx_g, lnx_b,
              lam_re, lam_im, log_dt, b_re, b_im, c_re, c_im, d_skip, w_glu, b_glu,
              p_a, p_b, w_out, final_g):
    f32 = jnp.float32
    bsz, seq, _ = x.shape
    H, N = RWKV_HEADS, RWKV_HEAD_DIM

    def heads(t):
        return t.reshape(bsz, seq, H, N).astype(f32)

    for l in range(DEPTH):
        h = rms_norm(x, norm_g[l])
        z = h @ w_in[l]
        zs, gate_a, u, gate_b, m_a, m_b = jnp.split(z, SPLIT_POINTS, axis=-1)

        zs = zs + mu_shift[l] * (token_shift(zs) - zs)
        r, k, v, xw, xa = jnp.split(zs, RWKV_SPLITS, axis=-1)
        w_log = -jax.nn.softplus(-(w0[l] + jnp.tanh(xw) @ w_up[l]).astype(f32)) - 0.5
        decay = jnp.exp(-jnp.exp(w_log))
        iclr = jax.nn.sigmoid((a0[l] + xa @ a_up[l]).astype(f32))
        iclr_h = iclr.reshape(bsz, seq, H, N)
        kk = heads(k * k_k[l])
        kk = kk / jnp.maximum(jnp.sqrt(jnp.sum(kk * kk, axis=-1, keepdims=True)), 1e-12)
        ka = k_a[l].reshape(H, N).astype(f32)
        k_h = heads(k) * (1.0 + (iclr_h - 1.0) * ka)
        r_h = heads(r)
        v_h = heads(v)
        y = wkv7(r_h, decay.reshape(bsz, seq, H, N), k_h, v_h, -kk, kk * iclr_h)
        mu = jnp.mean(y, axis=-1, keepdims=True)
        var = jnp.mean(jnp.square(y - mu), axis=-1, keepdims=True)
        y = (y - mu) * lax.rsqrt(var + LNX_EPS)
        y = y * lnx_g[l].reshape(H, N).astype(f32) + lnx_b[l].reshape(H, N).astype(f32)
        y = y + jnp.sum(r_h * k_h * r_k[l].astype(f32), axis=-1, keepdims=True) * v_h
        y_a = y.reshape(bsz, seq, RWKV_WIDTH).astype(x.dtype) * jax.nn.silu(gate_a)

        ys = s5_scan(u.reshape(bsz, seq, S5_GROUPS, S5_GROUP_DIM).astype(f32),
                     lam_re[l], lam_im[l], log_dt[l], b_re[l], b_im[l], c_re[l], c_im[l], d_skip[l])
        ys = jax.nn.gelu(ys.reshape(bsz, seq, S5_WIDTH).astype(x.dtype))
        g1, g2 = jnp.split(ys @ w_glu[l] + b_glu[l], 2, axis=-1)
        y_b = g1 * jax.nn.sigmoid(g2) * jax.nn.silu(gate_b)

        merged = jax.nn.sigmoid(m_a) * (y_a @ p_a[l]) + jax.nn.sigmoid(m_b) * (y_b @ p_b[l])
        x = x + merged @ w_out[l]
    return rms_norm(x, final_g)
```

```python
import functools
import math

import jax
import jax.numpy as jnp
from jax import lax
from jax.experimental import pallas as pl
from jax.experimental.pallas import tpu as pltpu

F32 = jnp.float32
BF16 = jnp.bfloat16

D_MODEL = 1024
HEADS = 16
HEAD_DIM = 64
RWKV_WIDTH = HEADS * HEAD_DIM
LORA = 64
S5_GROUPS = 32
S5_GROUP_DIM = 16
S5_WIDTH = S5_GROUPS * S5_GROUP_DIM
S5_STATE = 64
SHIFT_WIDTH = 3 * RWKV_WIDTH + 2 * LORA
IN_WIDTH = SHIFT_WIDTH + RWKV_WIDTH + 2 * S5_WIDTH + 2 * D_MODEL
RMS_EPS = 1e-6
LNX_EPS = 64e-5

CHUNK = 64
SEG_TILE = 256
S5_HALF = S5_WIDTH // 2
S5_HALF_STATE = (S5_GROUPS // 2) * S5_STATE
SCAN_ROWS = 8
SCAN_LANES = 512

VMEM_LIMIT_BYTES = 56 * 1024 * 1024

NT_DIMS = (((1,), (1,)), ((), ()))
TN_DIMS = (((0,), (0,)), ((), ()))


def _split_bf16(x):
    hi = x.astype(BF16)
    lo = (x - hi.astype(F32)).astype(BF16)
    return hi, lo


def _mm(a, b, passes, dims=None):
    if dims is None:
        dims = (((a.ndim - 1,), (0,)), ((), ()))
    dg = functools.partial(lax.dot_general, dimension_numbers=dims, preferred_element_type=F32)
    if passes == 1:
        return dg(a.astype(BF16), b.astype(BF16))
    if passes == 3:
        ah, al = _split_bf16(a)
        bh, bl = _split_bf16(b)
        return dg(ah, bh) + (dg(ah, bl) + dg(al, bh))
    return dg(a, b, precision=lax.Precision.HIGHEST)


def _iota(shape, dim):
    return lax.broadcasted_iota(jnp.int32, shape, dim)


def _seg_sum(x, ones_bd):
    hi, lo = _split_bf16(x)
    cols = []
    for j in range(x.shape[1] // SEG_TILE):
        sl = slice(j * SEG_TILE, (j + 1) * SEG_TILE)
        cols.append(jnp.dot(hi[:, sl], ones_bd, preferred_element_type=F32)
                    + jnp.dot(lo[:, sl], ones_bd, preferred_element_type=F32))
    return jnp.concatenate(cols, axis=1)


def _rms_norm(x, g):
    return x * lax.rsqrt(jnp.mean(x * x, axis=-1, keepdims=True) + RMS_EPS) * g


_IN_SPLITS = (0, SHIFT_WIDTH, SHIFT_WIDTH + RWKV_WIDTH, SHIFT_WIDTH + RWKV_WIDTH + S5_WIDTH,
              SHIFT_WIDTH + RWKV_WIDTH + 2 * S5_WIDTH, IN_WIDTH)


def _inproj_kernel(x_ref, g_ref, w_ref, zs_ref, ga_ref, u_ref, gb_ref, m_ref):
    h = _rms_norm(x_ref[...], g_ref[...]).astype(BF16)
    outs = (zs_ref, ga_ref, u_ref, gb_ref, m_ref)
    for o_ref, lo, hi in zip(outs, _IN_SPLITS[:-1], _IN_SPLITS[1:]):
        o_ref[...] = jnp.dot(h, w_ref[:, lo:hi], preferred_element_type=F32)


def _inproj(x2, norm_g, w_in_bf16, tm):
    t = x2.shape[0]
    widths = [hi - lo for lo, hi in zip(_IN_SPLITS[:-1], _IN_SPLITS[1:])]
    const = lambda i: (0, 0)
    return pl.pallas_call(
        _inproj_kernel,
        grid=(t // tm,),
        in_specs=[pl.BlockSpec((tm, D_MODEL), lambda i: (i, 0)),
                  pl.BlockSpec((1, D_MODEL), const),
                  pl.BlockSpec((D_MODEL, IN_WIDTH), const, pipeline_mode=pl.Buffered(1))],
        out_specs=[pl.BlockSpec((tm, w), lambda i: (i, 0)) for w in widths],
        out_shape=[jax.ShapeDtypeStruct((t, w), F32) for w in widths],
        compiler_params=pltpu.CompilerParams(dimension_semantics=("arbitrary",),
                                             vmem_limit_bytes=VMEM_LIMIT_BYTES),
        name="inproj",
    )(x2, norm_g, w_in_bf16)


def _wkv_kernel(zs_ref, mu_ref, w0_ref, wup_ref, a0_ref, aup_ref, kk_ref, ka_ref, rk_ref,
                lng_ref, lnb_ref, y_ref,
                prev_ref, h_ref, rt_ref, at_ref, bt_ref, kt_ref, bh_ref, kh_ref, v_ref, gc_ref, yc_ref,
                *, tt, p_small, p_chunk, p_state):
    n_chunks = tt // CHUNK

    @pl.when(pl.program_id(1) == 0)
    def _():
        prev_ref[...] = jnp.zeros_like(prev_ref)
        h_ref[...] = jnp.zeros_like(h_ref)

    z = zs_ref[...]
    row = _iota(z.shape, 0)
    z_prev = jnp.where(row == 0, prev_ref[...], pltpu.roll(z, 1, 0))
    prev_ref[...] = z[tt - 1:tt, :]
    zs = z + mu_ref[...] * (z_prev - z)
    r = zs[:, 0:RWKV_WIDTH]
    k = zs[:, RWKV_WIDTH:2 * RWKV_WIDTH]
    v = zs[:, 2 * RWKV_WIDTH:3 * RWKV_WIDTH]
    xw = zs[:, 3 * RWKV_WIDTH:3 * RWKV_WIDTH + LORA]
    xa = zs[:, 3 * RWKV_WIDTH + LORA:SHIFT_WIDTH]

    wl = w0_ref[...] + _mm(jnp.tanh(xw), wup_ref[...], p_small)
    neg = -wl
    softplus = jnp.maximum(neg, 0.0) + jnp.log1p(jnp.exp(-jnp.abs(neg)))
    logw = -jnp.exp(-softplus - 0.5)
    iclr = jax.nn.sigmoid(a0_ref[...] + _mm(xa, aup_ref[...], p_small))

    seg = (_iota((SEG_TILE, SEG_TILE), 0) // HEAD_DIM) == (_iota((SEG_TILE, SEG_TILE), 1) // HEAD_DIM)
    ones_bd = jnp.where(seg, 1.0, 0.0).astype(BF16)

    kk = k * kk_ref[...]
    kk = kk / jnp.maximum(jnp.sqrt(_seg_sum(kk * kk, ones_bd)), 1e-12)
    k_h = k * (1.0 + (iclr - 1.0) * ka_ref[...])
    a = -kk
    b = kk * iclr

    ti = _iota((tt, tt), 0)
    si = _iota((tt, tt), 1)
    same = (ti // CHUNK) == (si // CHUNK)
    tril_bd = jnp.where(same & (si <= ti), 1.0, 0.0)
    ones_tt = jnp.where(same, 1.0, 0.0)
    cum = _mm(tril_bd, logw, 6)
    cum_c = _mm(ones_tt, logw, 6)
    g_incl = jnp.exp(cum)
    g_excl = jnp.exp(cum - logw)
    g_inv = jnp.exp(-cum)
    g_c = jnp.exp(cum_c)

    rt_ref[...] = r * g_incl
    at_ref[...] = a * g_excl
    b_t = b * g_inv
    k_t = k_h * g_inv
    bt_ref[...] = b_t
    kt_ref[...] = k_t
    bh_ref[...] = b_t * g_c
    kh_ref[...] = k_t * g_c
    v_ref[...] = v
    gc_ref[...] = g_c

    gi = _iota((2 * CHUNK, 2 * CHUNK), 0)
    gj = _iota((2 * CHUNK, 2 * CHUNK), 1) % CHUNK
    g_mask = gj <= jnp.where(gi < CHUNK, gi, gi - (CHUNK + 1))
    eye = _iota((CHUNK, CHUNK), 0) == _iota((CHUNK, CHUNK), 1)
    eye_f = jnp.where(eye, 1.0, 0.0)
    zeros_cc = jnp.zeros((CHUNK, CHUNK), F32)

    def chunk_body(c, carry):
        rows = pl.ds(pl.multiple_of(c * CHUNK, CHUNK), CHUNK)
        for h in range(HEADS):
            lanes = slice(h * HEAD_DIM, (h + 1) * HEAD_DIM)
            rt = rt_ref[rows, lanes]
            at = at_ref[rows, lanes]
            bt = bt_ref[rows, lanes]
            kt = kt_ref[rows, lanes]
            bh = bh_ref[rows, lanes]
            kh = kh_ref[rows, lanes]
            vv = v_ref[rows, lanes]
            gc = gc_ref[pl.ds(pl.multiple_of(c * CHUNK, CHUNK), 8), lanes][0:1, :]

            g = _mm(jnp.concatenate([rt, at], axis=0), jnp.concatenate([bt, kt], axis=0),
                    p_chunk, NT_DIMS)
            g = jnp.where(g_mask, g, 0.0)
            a_rb = g[:CHUNK, :CHUNK]
            n = g[CHUNK:, :CHUNK]
            m = _mm(n, n, p_chunk)
            t_inv = eye_f + n
            for step in range(4):
                zz = _mm(m, jnp.concatenate([m, t_inv], axis=1), p_chunk)
                m = zz[:, :CHUNK]
                t_inv = t_inv + zz[:, CHUNK:]
            t_inv = t_inv + _mm(m, t_inv, p_chunk)

            gv = _mm(g, jnp.concatenate([zeros_cc, vv], axis=0), p_chunk)
            y0 = gv[:CHUNK]
            au = _mm(t_inv, jnp.concatenate([at, gv[CHUNK:]], axis=1), p_chunk)
            a_p = au[:, :CHUNK]
            u0 = au[:, CHUNK:]
            ra = _mm(a_rb, au, p_chunk)
            r_p = rt + ra[:, :CHUNK]
            y0 = y0 + ra[:, CHUNK:]
            phi = _mm(bh, a_p, p_chunk, TN_DIMS)
            phi = jnp.where(eye, phi + gc, phi)
            hk = _mm(jnp.concatenate([bh, kh], axis=0), jnp.concatenate([u0, vv], axis=0),
                     p_chunk, TN_DIMS)

            h0 = h_ref[h]
            zz = _mm(jnp.concatenate([r_p, phi], axis=0), h0, p_state)
            yc_ref[rows, lanes] = zz[:CHUNK] + y0
            h_ref[h] = zz[CHUNK:] + hk
        return carry

    lax.fori_loop(0, n_chunks, chunk_body, 0)

    y = yc_ref[...]
    mean = _seg_sum(y, ones_bd) * (1.0 / HEAD_DIM)
    yc = y - mean
    var = _seg_sum(yc * yc, ones_bd) * (1.0 / HEAD_DIM)
    yn = yc * lax.rsqrt(var + LNX_EPS) * lng_ref[...] + lnb_ref[...]
    bonus = _seg_sum(r * k_h * rk_ref[...], ones_bd)
    y_ref[...] = yn + bonus * v


def _wkv(zs, mu, w0, w_up, a0, a_up, k_k, k_a, r_k, lnx_g, lnx_b, bsz, seq, tt,
         p_small=3, p_chunk=6, p_state=6):
    n_t = seq // tt
    const = lambda b, i: (0, 0)
    row = lambda w: pl.BlockSpec((1, w), const)
    wide = pltpu.VMEM((tt, RWKV_WIDTH), F32)
    return pl.pallas_call(
        functools.partial(_wkv_kernel, tt=tt, p_small=p_small, p_chunk=p_chunk, p_state=p_state),
        grid=(bsz, n_t),
        in_specs=[pl.BlockSpec((tt, SHIFT_WIDTH), lambda b, i: (b * n_t + i, 0)),
                  row(SHIFT_WIDTH), row(RWKV_WIDTH),
                  pl.BlockSpec((LORA, RWKV_WIDTH), const),
                  row(RWKV_WIDTH),
                  pl.BlockSpec((LORA, RWKV_WIDTH), const),
                  row(RWKV_WIDTH), row(RWKV_WIDTH), row(RWKV_WIDTH), row(RWKV_WIDTH), row(RWKV_WIDTH)],
        out_specs=pl.BlockSpec((tt, RWKV_WIDTH), lambda b, i: (b * n_t + i, 0)),
        out_shape=jax.ShapeDtypeStruct((bsz * seq, RWKV_WIDTH), F32),
        scratch_shapes=[pltpu.VMEM((1, SHIFT_WIDTH), F32),
                        pltpu.VMEM((HEADS, HEAD_DIM, HEAD_DIM), F32),
                        wide, wide, wide, wide, wide, wide, wide, wide, wide],
        compiler_params=pltpu.CompilerParams(dimension_semantics=("arbitrary", "arbitrary"),
                                             vmem_limit_bytes=VMEM_LIMIT_BYTES),
        name="wkv7",
    )(zs, mu, w0, w_up, a0, a_up, k_k, k_a, r_k, lnx_g, lnx_b)


def _s5_kernel(u_ref, bm_ref, cm_ref, pw_ref, d_ref, y_ref, carry_ref, st_ref, *, tt, p_in, p_out):
    @pl.when(pl.program_id(1) == 0)
    def _():
        carry_ref[...] = jnp.zeros_like(carry_ref)

    u = u_ref[...]
    hs = S5_HALF_STATE
    for half in range(2):
        st_ref[:, half * 2 * hs:(half + 1) * 2 * hs] = _mm(
            u[:, half * S5_HALF:(half + 1) * S5_HALF], bm_ref[half], p_in)

    row8 = _iota((SCAN_ROWS, SCAN_LANES), 0)
    n_blocks = tt // SCAN_ROWS
    for half in range(2):
        for q in range(hs // SCAN_LANES):
            re_l = pl.ds(half * 2 * hs + q * SCAN_LANES, SCAN_LANES)
            im_l = pl.ds(half * 2 * hs + hs + q * SCAN_LANES, SCAN_LANES)
            pw_re = pw_ref[:, re_l]
            pw_im = pw_ref[:, im_l]

            def block(i, carry, re_l=re_l, im_l=im_l, pw_re=pw_re, pw_im=pw_im):
                c_re, c_im = carry
                rows = pl.ds(pl.multiple_of(i * SCAN_ROWS, SCAN_ROWS), SCAN_ROWS)
                x_re = st_ref[rows, re_l]
                x_im = st_ref[rows, im_l]
                for s in (1, 2, 4):
                    l_re = pw_re[s - 1:s, :]
                    l_im = pw_im[s - 1:s, :]
                    s_re = jnp.where(row8 >= s, pltpu.roll(x_re, s, 0), 0.0)
                    s_im = jnp.where(row8 >= s, pltpu.roll(x_im, s, 0), 0.0)
                    x_re, x_im = (x_re + (l_re * s_re - l_im * s_im),
                                  x_im + (l_re * s_im + l_im * s_re))
                x_re, x_im = (x_re + (pw_re * c_re - pw_im * c_im),
                              x_im + (pw_re * c_im + pw_im * c_re))
                st_ref[rows, re_l] = x_re
                st_ref[rows, im_l] = x_im
                return x_re[SCAN_ROWS - 1:SCAN_ROWS, :], x_im[SCAN_ROWS - 1:SCAN_ROWS, :]

            c_re, c_im = lax.fori_loop(0, n_blocks, block, (carry_ref[0:1, re_l], carry_ref[0:1, im_l]))
            carry_ref[0:1, re_l] = c_re
            carry_ref[0:1, im_l] = c_im

    outs = []
    for half in range(2):
        outs.append(_mm(st_ref[:, half * 2 * hs:(half + 1) * 2 * hs], cm_ref[half], p_out))
    y_ref[...] = jnp.concatenate(outs, axis=1) + d_ref[...] * u


def _s5(u, bmat, cmat, pw, d_row, bsz, seq, tt, p_in=3, p_out=3):
    n_t = seq // tt
    hs = S5_HALF_STATE
    const2 = lambda b, i: (0, 0)
    const3 = lambda b, i: (0, 0, 0)
    return pl.pallas_call(
        functools.partial(_s5_kernel, tt=tt, p_in=p_in, p_out=p_out),
        grid=(bsz, n_t),
        in_specs=[pl.BlockSpec((tt, S5_WIDTH), lambda b, i: (b * n_t + i, 0)),
                  pl.BlockSpec((2, S5_HALF, 2 * hs), const3),
                  pl.BlockSpec((2, 2 * hs, S5_HALF), const3),
                  pl.BlockSpec((SCAN_ROWS, 4 * hs), const2),
                  pl.BlockSpec((1, S5_WIDTH), const2)],
        out_specs=pl.BlockSpec((tt, S5_WIDTH), lambda b, i: (b * n_t + i, 0)),
        out_shape=jax.ShapeDtypeStruct((bsz * seq, S5_WIDTH), F32),
        scratch_shapes=[pltpu.VMEM((SCAN_ROWS, 4 * hs), F32),
                        pltpu.VMEM((tt, 4 * hs), F32)],
        compiler_params=pltpu.CompilerParams(dimension_semantics=("arbitrary", "arbitrary"),
                                             vmem_limit_bytes=VMEM_LIMIT_BYTES),
        name="s5",
    )(u, bmat, cmat, pw, d_row)


def _s5_params(lam_re, lam_im, log_dt, b_re, b_im, c_re, c_im):
    dt = jnp.exp(log_dt)[:, None]
    mag = jnp.exp(lam_re * dt)
    ang = lam_im * dt
    ab_re = mag * jnp.cos(ang)
    ab_im = mag * jnp.sin(ang)
    den = lam_re * lam_re + lam_im * lam_im
    nr = ab_re - 1.0
    f_re = (nr * lam_re + ab_im * lam_im) / den
    f_im = (ab_im * lam_re - nr * lam_im) / den
    bb_re = f_re[..., None] * b_re - f_im[..., None] * b_im
    bb_im = f_re[..., None] * b_im + f_im[..., None] * b_re

    gh = S5_GROUPS // 2
    eye = jnp.eye(gh, dtype=F32)

    def in_half(bb):
        return jnp.einsum('gph,gk->ghkp', bb, eye).reshape(gh * S5_GROUP_DIM, gh * S5_STATE)

    def out_half(cc):
        return jnp.einsum('ghp,gk->gpkh', cc, eye).reshape(gh * S5_STATE, gh * S5_GROUP_DIM)

    bmat = jnp.stack([jnp.concatenate([in_half(bb_re[s]), in_half(bb_im[s])], axis=1)
                      for s in (slice(0, gh), slice(gh, None))])
    cmat = jnp.stack([jnp.concatenate([out_half(c_re[s]), -out_half(c_im[s])], axis=0)
                      for s in (slice(0, gh), slice(gh, None))])

    steps = jnp.arange(1, SCAN_ROWS + 1, dtype=F32)[:, None, None]
    pmag = jnp.exp(steps * (lam_re * dt)[None])
    pang = steps * ang[None]
    p_re = (pmag * jnp.cos(pang)).reshape(SCAN_ROWS, 2, gh * S5_STATE)
    p_im = (pmag * jnp.sin(pang)).reshape(SCAN_ROWS, 2, gh * S5_STATE)
    pw = jnp.concatenate([p_re[:, 0], p_im[:, 0], p_re[:, 1], p_im[:, 1]], axis=1)
    return bmat, cmat, pw


def _out_kernel(x_ref, ya_ref, ga_ref, ys_ref, gb_ref, m_ref, wglu_ref, bglu_ref, pa_ref, pb_ref,
                wout_ref, fg_ref, o_ref):
    y_a = ya_ref[...] * jax.nn.silu(ga_ref[...])
    proj_a = jnp.dot(y_a.astype(BF16), pa_ref[...], preferred_element_type=F32)
    ys = jax.nn.gelu(ys_ref[...])
    glu = jnp.dot(ys.astype(BF16), wglu_ref[...], preferred_element_type=F32) + bglu_ref[...]
    y_b = glu[:, :S5_WIDTH] * jax.nn.sigmoid(glu[:, S5_WIDTH:]) * jax.nn.silu(gb_ref[...])
    proj_b = jnp.dot(y_b.astype(BF16), pb_ref[...], preferred_element_type=F32)
    m = m_ref[...]
    merged = jax.nn.sigmoid(m[:, :D_MODEL]) * proj_a + jax.nn.sigmoid(m[:, D_MODEL:]) * proj_b
    x_new = x_ref[...] + jnp.dot(merged.astype(BF16), wout_ref[...], preferred_element_type=F32)
    o_ref[...] = _rms_norm(x_new, fg_ref[...])


def _out(x2, ya, ga, ys, gb, m, w_glu, b_glu, p_a, p_b, w_out, final_g, tm):
    t = x2.shape[0]
    const = lambda i: (0, 0)
    tile = lambda w: pl.BlockSpec((tm, w), lambda i: (i, 0))
    full = lambda a: pl.BlockSpec(a.shape, const)
    return pl.pallas_call(
        _out_kernel,
        grid=(t // tm,),
        in_specs=[tile(D_MODEL), tile(RWKV_WIDTH), tile(RWKV_WIDTH), tile(S5_WIDTH), tile(S5_WIDTH),
                  tile(2 * D_MODEL), full(w_glu), full(b_glu), full(p_a), full(p_b), full(w_out),
                  full(final_g)],
        out_specs=tile(D_MODEL),
        out_shape=jax.ShapeDtypeStruct((t, D_MODEL), F32),
        compiler_params=pltpu.CompilerParams(dimension_semantics=("arbitrary",),
                                             vmem_limit_bytes=VMEM_LIMIT_BYTES),
        name="merge_out",
    )(x2, ya, ga, ys, gb, m, w_glu, b_glu, p_a, p_b, w_out, final_g)


def kernel(x, norm_g, w_in, mu_shift, w0, w_up, a0, a_up, k_k, k_a, r_k, lnx_g, lnx_b, lam_re, lam_im,
           log_dt, b_re, b_im, c_re, c_im, d_skip, w_glu, b_glu, p_a, p_b, w_out, final_g):
    bsz, seq, _ = x.shape
    assert norm_g.shape[0] == 1, "single-layer block: the final RMSNorm is fused into the output kernel"
    tm = min(256, seq)
    tt_wkv = min(256, seq)
    tt_s5 = min(512, seq)
    x2 = x.reshape(bsz * seq, D_MODEL)
    row = lambda p: p.reshape(1, -1).astype(F32)
    l = 0
    zs, ga, u, gb, m = _inproj(x2, row(norm_g[l]), w_in[l].astype(BF16), tm)
    ya = _wkv(zs, row(mu_shift[l]), row(w0[l]), w_up[l], row(a0[l]), a_up[l], row(k_k[l]),
              row(k_a[l]), row(r_k[l]), row(lnx_g[l]), row(lnx_b[l]), bsz, seq, tt_wkv)
    bmat, cmat, pw = _s5_params(lam_re[l], lam_im[l], log_dt[l], b_re[l], b_im[l], c_re[l], c_im[l])
    ys = _s5(u, bmat, cmat, pw, row(d_skip[l]), bsz, seq, tt_s5)
    out = _out(x2, ya, ga, ys, gb, m, w_glu[l].astype(BF16), row(b_glu[l]), p_a[l].astype(BF16),
               p_b[l].astype(BF16), w_out[l].astype(BF16), row(final_g), tm)
    return out.reshape(bsz, seq, D_MODEL)
```

```python
import functools
import math

import jax
import jax.numpy as jnp
from jax import lax
from jax.experimental import pallas as pl
from jax.experimental.pallas import tpu as pltpu

F32 = jnp.float32
BF16 = jnp.bfloat16

D_MODEL = 1024
HEADS = 16
HEAD_DIM = 64
RWKV_WIDTH = HEADS * HEAD_DIM
LORA = 64
S5_GROUPS = 32
S5_GROUP_DIM = 16
S5_WIDTH = S5_GROUPS * S5_GROUP_DIM
S5_STATE = 64
SHIFT_WIDTH = 3 * RWKV_WIDTH + 2 * LORA
IN_WIDTH = SHIFT_WIDTH + RWKV_WIDTH + 2 * S5_WIDTH + 2 * D_MODEL
RMS_EPS = 1e-6
LNX_EPS = 64e-5

CHUNK = 64
PAIR = 2 * HEAD_DIM
HEAD_GROUP = 8
SEG_TILE = 256
S5_HALF = S5_WIDTH // 2
S5_HALF_STATE = (S5_GROUPS // 2) * S5_STATE
SCAN_ROWS = 8
SCAN_LANES = 512

VMEM_LIMIT_BYTES = 56 * 1024 * 1024

NT_DIMS = (((1,), (1,)), ((), ()))
TN_DIMS = (((0,), (0,)), ((), ()))


def _split_bf16(x):
    hi = x.astype(BF16)
    lo = (x - hi.astype(F32)).astype(BF16)
    return hi, lo


def _mm(a, b, passes, dims=None):
    if dims is None:
        dims = (((a.ndim - 1,), (0,)), ((), ()))
    dg = functools.partial(lax.dot_general, dimension_numbers=dims, preferred_element_type=F32)
    if passes == 1:
        return dg(a.astype(BF16), b.astype(BF16))
    if passes == 3:
        ah, al = _split_bf16(a)
        bh, bl = _split_bf16(b)
        return dg(ah, bh) + (dg(ah, bl) + dg(al, bh))
    return dg(a, b, precision=lax.Precision.HIGHEST)


def _iota(shape, dim):
    return lax.broadcasted_iota(jnp.int32, shape, dim)


def _seg_sum(x, ones_bd):
    hi, lo = _split_bf16(x)
    cols = []
    for j in range(x.shape[1] // SEG_TILE):
        sl = slice(j * SEG_TILE, (j + 1) * SEG_TILE)
        cols.append(jnp.dot(hi[:, sl], ones_bd, preferred_element_type=F32)
                    + jnp.dot(lo[:, sl], ones_bd, preferred_element_type=F32))
    return jnp.concatenate(cols, axis=1)


def _swap_halves(x):
    n = x.shape[1]
    low = (_iota(x.shape, 1) % PAIR) < HEAD_DIM
    return jnp.where(low, pltpu.roll(x, n - HEAD_DIM, 1), pltpu.roll(x, HEAD_DIM, 1))


def _rms_norm(x, g):
    return x * lax.rsqrt(jnp.mean(x * x, axis=-1, keepdims=True) + RMS_EPS) * g


_IN_SPLITS = (0, SHIFT_WIDTH, SHIFT_WIDTH + RWKV_WIDTH, SHIFT_WIDTH + RWKV_WIDTH + S5_WIDTH,
              SHIFT_WIDTH + RWKV_WIDTH + 2 * S5_WIDTH, IN_WIDTH)


def _inproj_kernel(x_ref, g_ref, w_ref, zs_ref, ga_ref, u_ref, gb_ref, m_ref):
    h = _rms_norm(x_ref[...], g_ref[...]).astype(BF16)
    outs = (zs_ref, ga_ref, u_ref, gb_ref, m_ref)
    for o_ref, lo, hi in zip(outs, _IN_SPLITS[:-1], _IN_SPLITS[1:]):
        o_ref[...] = jnp.dot(h, w_ref[:, lo:hi], preferred_element_type=F32)


def _inproj(x2, norm_g, w_in_bf16, tm):
    t = x2.shape[0]
    widths = [hi - lo for lo, hi in zip(_IN_SPLITS[:-1], _IN_SPLITS[1:])]
    const = lambda i: (0, 0)
    return pl.pallas_call(
        _inproj_kernel,
        grid=(t // tm,),
        in_specs=[pl.BlockSpec((tm, D_MODEL), lambda i: (i, 0)),
                  pl.BlockSpec((1, D_MODEL), const),
                  pl.BlockSpec((D_MODEL, IN_WIDTH), const, pipeline_mode=pl.Buffered(1))],
        out_specs=[pl.BlockSpec((tm, w), lambda i: (i, 0)) for w in widths],
        out_shape=[jax.ShapeDtypeStruct((t, w), F32) for w in widths],
        compiler_params=pltpu.CompilerParams(dimension_semantics=("arbitrary",),
                                             vmem_limit_bytes=VMEM_LIMIT_BYTES),
        name="inproj",
    )(x2, norm_g, w_in_bf16)


def _wkv_kernel(zs_ref, mu_ref, w0_ref, wup_ref, a0_ref, aup_ref, kk_ref, ka_ref, rk_ref,
                lng_ref, lnb_ref, y_ref,
                prev_ref, h_ref, rt_ref, at_ref, bt_ref, kt_ref, bh_ref, kh_ref, v_ref, gc_ref, yc_ref,
                *, tt, p_small, p_chunk, p_state):
    n_chunks = tt // CHUNK

    @pl.when(pl.program_id(1) == 0)
    def _():
        prev_ref[...] = jnp.zeros_like(prev_ref)
        h_ref[...] = jnp.zeros_like(h_ref)

    z = zs_ref[...]
    row = _iota(z.shape, 0)
    z_prev = jnp.where(row == 0, prev_ref[...], pltpu.roll(z, 1, 0))
    prev_ref[...] = z[tt - 1:tt, :]
    zs = z + mu_ref[...] * (z_prev - z)
    r = zs[:, 0:RWKV_WIDTH]
    k = zs[:, RWKV_WIDTH:2 * RWKV_WIDTH]
    v = zs[:, 2 * RWKV_WIDTH:3 * RWKV_WIDTH]
    xw = zs[:, 3 * RWKV_WIDTH:3 * RWKV_WIDTH + LORA]
    xa = zs[:, 3 * RWKV_WIDTH + LORA:SHIFT_WIDTH]

    wl = w0_ref[...] + _mm(jnp.tanh(xw), wup_ref[...], p_small)
    neg = -wl
    softplus = jnp.maximum(neg, 0.0) + jnp.log1p(jnp.exp(-jnp.abs(neg)))
    logw = -jnp.exp(-softplus - 0.5)
    iclr = jax.nn.sigmoid(a0_ref[...] + _mm(xa, aup_ref[...], p_small))

    seg = (_iota((SEG_TILE, SEG_TILE), 0) // HEAD_DIM) == (_iota((SEG_TILE, SEG_TILE), 1) // HEAD_DIM)
    ones_bd = jnp.where(seg, 1.0, 0.0).astype(BF16)

    kk = k * kk_ref[...]
    kk = kk / jnp.maximum(jnp.sqrt(_seg_sum(kk * kk, ones_bd)), 1e-12)
    k_h = k * (1.0 + (iclr - 1.0) * ka_ref[...])
    a = -kk
    b = kk * iclr

    ti = _iota((tt, tt), 0)
    si = _iota((tt, tt), 1)
    same = (ti // CHUNK) == (si // CHUNK)
    tril_bd = jnp.where(same & (si <= ti), 1.0, 0.0)
    ones_tt = jnp.where(same, 1.0, 0.0)
    cum = _mm(tril_bd, logw, 6)
    cum_c = _mm(ones_tt, logw, 6)
    g_incl = jnp.exp(cum)
    g_excl = jnp.exp(cum - logw)
    g_inv = jnp.exp(-cum)
    g_c = jnp.exp(cum_c)

    rt_ref[...] = r * g_incl
    at_ref[...] = a * g_excl
    b_t = b * g_inv
    k_t = k_h * g_inv
    bt_ref[...] = b_t
    kt_ref[...] = k_t
    bh_ref[...] = b_t * g_c
    kh_ref[...] = k_t * g_c
    v_ref[...] = _swap_halves(v)
    gc_ref[...] = g_c

    gi = _iota((2 * CHUNK, 2 * CHUNK), 0)
    gj = _iota((2 * CHUNK, 2 * CHUNK), 1) % CHUNK
    g_mask = gj <= jnp.where(gi < CHUNK, gi, gi - (CHUNK + 1))
    lane = _iota((CHUNK, PAIR), 1)
    low = lane < HEAD_DIM
    own = (low, jnp.logical_not(low))
    diag = tuple(lane == _iota((CHUNK, PAIR), 0) + s * HEAD_DIM for s in range(2))
    zeros_cp = jnp.zeros((CHUNK, PAIR), F32)
    cast = (lambda t: t.astype(BF16)) if p_chunk == 1 else (lambda t: t)

    def chunk_body(c, carry):
        rows = pl.ds(pl.multiple_of(c * CHUNK, CHUNK), CHUNK)
        for grp in range(HEADS // HEAD_GROUP):
            heads = range(grp * HEAD_GROUP, (grp + 1) * HEAD_GROUP)
            pairs = range(grp * HEAD_GROUP // 2, (grp + 1) * HEAD_GROUP // 2)
            ld = lambda ref, p: ref[rows, p * PAIR:(p + 1) * PAIR]
            rt = {p: ld(rt_ref, p) for p in pairs}
            at = {p: ld(at_ref, p) for p in pairs}
            vs = {p: ld(v_ref, p) for p in pairs}
            bk = {p: cast(jnp.concatenate([ld(bt_ref, p), ld(kt_ref, p)], axis=0)) for p in pairs}
            bkt = {p: cast(jnp.concatenate([ld(bh_ref, p), ld(kh_ref, p)], axis=0).T) for p in pairs}

            g, x, m, vo = {}, {}, {}, {}
            for h in heads:
                p, s = divmod(h, 2)
                lhs = jnp.concatenate([jnp.where(own[s], rt[p], 0.0), jnp.where(own[s], at[p], 0.0)], axis=0)
                g[h] = cast(jnp.where(g_mask, _mm(cast(lhs), bk[p], p_chunk, NT_DIMS), 0.0))
            for h in heads:
                p, s = divmod(h, 2)
                vo[h] = cast(jnp.where(own[s], 0.0, vs[p]))
                vak = _mm(g[h][CHUNK:, :], jnp.concatenate([cast(zeros_cp), vo[h]], axis=0), p_chunk)
                x[h] = jnp.where(own[s], at[p], 0.0) + vak
                m[h] = g[h][CHUNK:, :CHUNK]
            for step in range(5):
                for h in heads:
                    zz = _mm(m[h], jnp.concatenate([cast(x[h]), m[h]], axis=1), p_chunk)
                    x[h] = x[h] + zz[:, :PAIR]
                    m[h] = cast(zz[:, PAIR:])
            for h in heads:
                x[h] = x[h] + _mm(m[h], cast(x[h]), p_chunk)
            o9 = {}
            for h in heads:
                p, s = divmod(h, 2)
                lhs = jnp.concatenate([g[h][:CHUNK, :], bkt[p][s * CHUNK:(s + 1) * CHUNK, :]], axis=0)
                o9[h] = _mm(lhs, jnp.concatenate([cast(x[h]), vo[h]], axis=0), p_chunk)
            for p in pairs:
                t0, t1 = o9[2 * p][:CHUNK], o9[2 * p + 1][:CHUNK]
                b0, b1 = o9[2 * p][CHUNK:], o9[2 * p + 1][CHUNK:]
                gc = gc_ref[pl.ds(pl.multiple_of(c * CHUNK, CHUNK), 8), p * PAIR:(p + 1) * PAIR][0:1, :]
                r_p = rt[p] + jnp.where(low, t0, t1)
                y0 = jnp.where(low, t1, t0)
                phi = jnp.concatenate([jnp.where(low, jnp.where(diag[0], b0 + gc, b0), 0.0),
                                       jnp.where(low, 0.0, jnp.where(diag[1], b1 + gc, b1))], axis=0)
                hk = jnp.concatenate([jnp.where(low, 0.0, b0), jnp.where(low, b1, 0.0)], axis=0)
                zz = _mm(jnp.concatenate([r_p, phi], axis=0), h_ref[p], p_state)
                yc_ref[rows, p * PAIR:(p + 1) * PAIR] = zz[:CHUNK] + y0
                h_ref[p] = zz[CHUNK:] + hk
        return carry

    lax.fori_loop(0, n_chunks, chunk_body, 0)

    y = _swap_halves(yc_ref[...])
    mean = _seg_sum(y, ones_bd) * (1.0 / HEAD_DIM)
    yc = y - mean
    var = _seg_sum(yc * yc, ones_bd) * (1.0 / HEAD_DIM)
    yn = yc * lax.rsqrt(var + LNX_EPS) * lng_ref[...] + lnb_ref[...]
    bonus = _seg_sum(r * k_h * rk_ref[...], ones_bd)
    y_ref[...] = yn + bonus * v


def _wkv(zs, mu, w0, w_up, a0, a_up, k_k, k_a, r_k, lnx_g, lnx_b, bsz, seq, tt,
         p_small=3, p_chunk=1, p_state=3):
    n_t = seq // tt
    const = lambda b, i: (0, 0)
    row = lambda w: pl.BlockSpec((1, w), const)
    wide = pltpu.VMEM((tt, RWKV_WIDTH), F32)
    return pl.pallas_call(
        functools.partial(_wkv_kernel, tt=tt, p_small=p_small, p_chunk=p_chunk, p_state=p_state),
        grid=(bsz, n_t),
        in_specs=[pl.BlockSpec((tt, SHIFT_WIDTH), lambda b, i: (b * n_t + i, 0)),
                  row(SHIFT_WIDTH), row(RWKV_WIDTH),
                  pl.BlockSpec((LORA, RWKV_WIDTH), const),
                  row(RWKV_WIDTH),
                  pl.BlockSpec((LORA, RWKV_WIDTH), const),
                  row(RWKV_WIDTH), row(RWKV_WIDTH), row(RWKV_WIDTH), row(RWKV_WIDTH), row(RWKV_WIDTH)],
        out_specs=pl.BlockSpec((tt, RWKV_WIDTH), lambda b, i: (b * n_t + i, 0)),
        out_shape=jax.ShapeDtypeStruct((bsz * seq, RWKV_WIDTH), F32),
        scratch_shapes=[pltpu.VMEM((1, SHIFT_WIDTH), F32),
                        pltpu.VMEM((HEADS // 2, PAIR, PAIR), F32),
                        wide, wide, wide, wide, wide, wide, wide, wide, wide],
        compiler_params=pltpu.CompilerParams(dimension_semantics=("arbitrary", "arbitrary"),
                                             vmem_limit_bytes=VMEM_LIMIT_BYTES),
        name="wkv7",
    )(zs, mu, w0, w_up, a0, a_up, k_k, k_a, r_k, lnx_g, lnx_b)


def _s5_kernel(u_ref, bm_ref, cm_ref, pw_ref, d_ref, y_ref, carry_ref, st_ref, *, tt, p_in, p_out):
    @pl.when(pl.program_id(1) == 0)
    def _():
        carry_ref[...] = jnp.zeros_like(carry_ref)

    u = u_ref[...]
    hs = S5_HALF_STATE
    for half in range(2):
        st_ref[:, half * 2 * hs:(half + 1) * 2 * hs] = _mm(
            u[:, half * S5_HALF:(half + 1) * S5_HALF], bm_ref[half], p_in)

    row8 = _iota((SCAN_ROWS, SCAN_LANES), 0)
    n_blocks = tt // SCAN_ROWS
    for half in range(2):
        for q in range(hs // SCAN_LANES):
            re_l = pl.ds(half * 2 * hs + q * SCAN_LANES, SCAN_LANES)
            im_l = pl.ds(half * 2 * hs + hs + q * SCAN_LANES, SCAN_LANES)
            pw_re = pw_ref[:, re_l]
            pw_im = pw_ref[:, im_l]

            def block(i, carry, re_l=re_l, im_l=im_l, pw_re=pw_re, pw_im=pw_im):
                c_re, c_im = carry
                rows = pl.ds(pl.multiple_of(i * SCAN_ROWS, SCAN_ROWS), SCAN_ROWS)
                x_re = st_ref[rows, re_l]
                x_im = st_ref[rows, im_l]
                for s in (1, 2, 4):
                    l_re = pw_re[s - 1:s, :]
                    l_im = pw_im[s - 1:s, :]
                    s_re = jnp.where(row8 >= s, pltpu.roll(x_re, s, 0), 0.0)
                    s_im = jnp.where(row8 >= s, pltpu.roll(x_im, s, 0), 0.0)
                    x_re, x_im = (x_re + (l_re * s_re - l_im * s_im),
                                  x_im + (l_re * s_im + l_im * s_re))
                x_re, x_im = (x_re + (pw_re * c_re - pw_im * c_im),
                              x_im + (pw_re * c_im + pw_im * c_re))
                st_ref[rows, re_l] = x_re
                st_ref[rows, im_l] = x_im
                return x_re[SCAN_ROWS - 1:SCAN_ROWS, :], x_im[SCAN_ROWS - 1:SCAN_ROWS, :]

            c_re, c_im = lax.fori_loop(0, n_blocks, block, (carry_ref[0:1, re_l], carry_ref[0:1, im_l]))
            carry_ref[0:1, re_l] = c_re
            carry_ref[0:1, im_l] = c_im

    outs = []
    for half in range(2):
        outs.append(_mm(st_ref[:, half * 2 * hs:(half + 1) * 2 * hs], cm_ref[half], p_out))
    y_ref[...] = jnp.concatenate(outs, axis=1) + d_ref[...] * u


def _s5(u, bmat, cmat, pw, d_row, bsz, seq, tt, p_in=3, p_out=3):
    n_t = seq // tt
    hs = S5_HALF_STATE
    const2 = lambda b, i: (0, 0)
    const3 = lambda b, i: (0, 0, 0)
    return pl.pallas_call(
        functools.partial(_s5_kernel, tt=tt, p_in=p_in, p_out=p_out),
        grid=(bsz, n_t),
        in_specs=[pl.BlockSpec((tt, S5_WIDTH), lambda b, i: (b * n_t + i, 0)),
                  pl.BlockSpec((2, S5_HALF, 2 * hs), const3),
                  pl.BlockSpec((2, 2 * hs, S5_HALF), const3),
                  pl.BlockSpec((SCAN_ROWS, 4 * hs), const2),
                  pl.BlockSpec((1, S5_WIDTH), const2)],
        out_specs=pl.BlockSpec((tt, S5_WIDTH), lambda b, i: (b * n_t + i, 0)),
        out_shape=jax.ShapeDtypeStruct((bsz * seq, S5_WIDTH), F32),
        scratch_shapes=[pltpu.VMEM((SCAN_ROWS, 4 * hs), F32),
                        pltpu.VMEM((tt, 4 * hs), F32)],
        compiler_params=pltpu.CompilerParams(dimension_semantics=("arbitrary", "arbitrary"),
                                             vmem_limit_bytes=VMEM_LIMIT_BYTES),
        name="s5",
    )(u, bmat, cmat, pw, d_row)


def _s5_params(lam_re, lam_im, log_dt, b_re, b_im, c_re, c_im):
    dt = jnp.exp(log_dt)[:, None]
    mag = jnp.exp(lam_re * dt)
    ang = lam_im * dt
    ab_re = mag * jnp.cos(ang)
    ab_im = mag * jnp.sin(ang)
    den = lam_re * lam_re + lam_im * lam_im
    nr = ab_re - 1.0
    f_re = (nr * lam_re + ab_im * lam_im) / den
    f_im = (ab_im * lam_re - nr * lam_im) / den
    bb_re = f_re[..., None] * b_re - f_im[..., None] * b_im
    bb_im = f_re[..., None] * b_im + f_im[..., None] * b_re

    gh = S5_GROUPS // 2
    eye = jnp.eye(gh, dtype=F32)

    def in_half(bb):
        return jnp.einsum('gph,gk->ghkp', bb, eye).reshape(gh * S5_GROUP_DIM, gh * S5_STATE)

    def out_half(cc):
        return jnp.einsum('ghp,gk->gpkh', cc, eye).reshape(gh * S5_STATE, gh * S5_GROUP_DIM)

    bmat = jnp.stack([jnp.concatenate([in_half(bb_re[s]), in_half(bb_im[s])], axis=1)
                      for s in (slice(0, gh), slice(gh, None))])
    cmat = jnp.stack([jnp.concatenate([out_half(c_re[s]), -out_half(c_im[s])], axis=0)
                      for s in (slice(0, gh), slice(gh, None))])

    steps = jnp.arange(1, SCAN_ROWS + 1, dtype=F32)[:, None, None]
    pmag = jnp.exp(steps * (lam_re * dt)[None])
    pang = steps * ang[None]
    p_re = (pmag * jnp.cos(pang)).reshape(SCAN_ROWS, 2, gh * S5_STATE)
    p_im = (pmag * jnp.sin(pang)).reshape(SCAN_ROWS, 2, gh * S5_STATE)
    pw = jnp.concatenate([p_re[:, 0], p_im[:, 0], p_re[:, 1], p_im[:, 1]], axis=1)
    return bmat, cmat, pw


def _out_kernel(x_ref, ya_ref, ga_ref, ys_ref, gb_ref, m_ref, wglu_ref, bglu_ref, pa_ref, pb_ref,
                wout_ref, fg_ref, o_ref):
    y_a = ya_ref[...] * jax.nn.silu(ga_ref[...])
    proj_a = jnp.dot(y_a.astype(BF16), pa_ref[...], preferred_element_type=F32)
    ys = jax.nn.gelu(ys_ref[...])
    glu = jnp.dot(ys.astype(BF16), wglu_ref[...], preferred_element_type=F32) + bglu_ref[...]
    y_b = glu[:, :S5_WIDTH] * jax.nn.sigmoid(glu[:, S5_WIDTH:]) * jax.nn.silu(gb_ref[...])
    proj_b = jnp.dot(y_b.astype(BF16), pb_ref[...], preferred_element_type=F32)
    m = m_ref[...]
    merged = jax.nn.sigmoid(m[:, :D_MODEL]) * proj_a + jax.nn.sigmoid(m[:, D_MODEL:]) * proj_b
    x_new = x_ref[...] + jnp.dot(merged.astype(BF16), wout_ref[...], preferred_element_type=F32)
    o_ref[...] = _rms_norm(x_new, fg_ref[...])


def _out(x2, ya, ga, ys, gb, m, w_glu, b_glu, p_a, p_b, w_out, final_g, tm):
    t = x2.shape[0]
    const = lambda i: (0, 0)
    tile = lambda w: pl.BlockSpec((tm, w), lambda i: (i, 0))
    full = lambda a: pl.BlockSpec(a.shape, const)
    return pl.pallas_call(
        _out_kernel,
        grid=(t // tm,),
        in_specs=[tile(D_MODEL), tile(RWKV_WIDTH), tile(RWKV_WIDTH), tile(S5_WIDTH), tile(S5_WIDTH),
                  tile(2 * D_MODEL), full(w_glu), full(b_glu), full(p_a), full(p_b), full(w_out),
                  full(final_g)],
        out_specs=tile(D_MODEL),
        out_shape=jax.ShapeDtypeStruct((t, D_MODEL), F32),
        compiler_params=pltpu.CompilerParams(dimension_semantics=("arbitrary",),
                                             vmem_limit_bytes=VMEM_LIMIT_BYTES),
        name="merge_out",
    )(x2, ya, ga, ys, gb, m, w_glu, b_glu, p_a, p_b, w_out, final_g)


def kernel(x, norm_g, w_in, mu_shift, w0, w_up, a0, a_up, k_k, k_a, r_k, lnx_g, lnx_b, lam_re, lam_im,
           log_dt, b_re, b_im, c_re, c_im, d_skip, w_glu, b_glu, p_a, p_b, w_out, final_g):
    bsz, seq, _ = x.shape
    assert norm_g.shape[0] == 1, "single-layer block: the final RMSNorm is fused into the output kernel"
    tm = min(256, seq)
    tt_wkv = min(256, seq)
    tt_s5 = min(512, seq)
    x2 = x.reshape(bsz * seq, D_MODEL)
    row = lambda p: p.reshape(1, -1).astype(F32)
    l = 0
    zs, ga, u, gb, m = _inproj(x2, row(norm_g[l]), w_in[l].astype(BF16), tm)
    ya = _wkv(zs, row(mu_shift[l]), row(w0[l]), w_up[l], row(a0[l]), a_up[l], row(k_k[l]),
              row(k_a[l]), row(r_k[l]), row(lnx_g[l]), row(lnx_b[l]), bsz, seq, tt_wkv)
    bmat, cmat, pw = _s5_params(lam_re[l], lam_im[l], log_dt[l], b_re[l], b_im[l], c_re[l], c_im[l])
    ys = _s5(u, bmat, cmat, pw, row(d_skip[l]), bsz, seq, tt_s5)
    out = _out(x2, ya, ga, ys, gb, m, w_glu[l].astype(BF16), row(b_glu[l]), p_a[l].astype(BF16),
               p_b[l].astype(BF16), w_out[l].astype(BF16), row(final_g), tm)
    return out.reshape(bsz, seq, D_MODEL)
```

```python
import functools
import math

import jax
import jax.numpy as jnp
from jax import lax
from jax.experimental import pallas as pl
from jax.experimental.pallas import tpu as pltpu

F32 = jnp.float32
BF16 = jnp.bfloat16

D_MODEL = 1024
HEADS = 16
HEAD_DIM = 64
RWKV_WIDTH = HEADS * HEAD_DIM
LORA = 64
S5_GROUPS = 32
S5_GROUP_DIM = 16
S5_WIDTH = S5_GROUPS * S5_GROUP_DIM
S5_STATE = 64
SHIFT_WIDTH = 3 * RWKV_WIDTH + 2 * LORA
IN_WIDTH = SHIFT_WIDTH + RWKV_WIDTH + 2 * S5_WIDTH + 2 * D_MODEL
RMS_EPS = 1e-6
LNX_EPS = 64e-5

CHUNK = 64
PAIR = 2 * HEAD_DIM
HEAD_GROUP = 16
SEG_TILE = 256
S5_HALF = S5_WIDTH // 2
S5_HALF_STATE = (S5_GROUPS // 2) * S5_STATE
SCAN_ROWS = 8
SCAN_LANES = 512

VMEM_LIMIT_BYTES = 56 * 1024 * 1024

NT_DIMS = (((1,), (1,)), ((), ()))
TN_DIMS = (((0,), (0,)), ((), ()))


def _split_bf16(x):
    hi = x.astype(BF16)
    lo = (x - hi.astype(F32)).astype(BF16)
    return hi, lo


def _mm(a, b, passes, dims=None):
    if dims is None:
        dims = (((a.ndim - 1,), (0,)), ((), ()))
    dg = functools.partial(lax.dot_general, dimension_numbers=dims, preferred_element_type=F32)
    if passes == 1:
        return dg(a.astype(BF16), b.astype(BF16))
    if passes == 3:
        ah, al = _split_bf16(a)
        bh, bl = _split_bf16(b)
        return dg(ah, bh) + (dg(ah, bl) + dg(al, bh))
    return dg(a, b, precision=lax.Precision.HIGHEST)


def _iota(shape, dim):
    return lax.broadcasted_iota(jnp.int32, shape, dim)


def _seg_sum(x, ones_bd):
    xb = x.astype(BF16)
    cols = [jnp.dot(xb[:, j * SEG_TILE:(j + 1) * SEG_TILE], ones_bd, preferred_element_type=F32)
            for j in range(x.shape[1] // SEG_TILE)]
    return jnp.concatenate(cols, axis=1)


def _swap_halves(x):
    n = x.shape[1]
    low = (_iota(x.shape, 1) % PAIR) < HEAD_DIM
    return jnp.where(low, pltpu.roll(x, n - HEAD_DIM, 1), pltpu.roll(x, HEAD_DIM, 1))


def _rms_norm(x, g):
    return x * lax.rsqrt(jnp.mean(x * x, axis=-1, keepdims=True) + RMS_EPS) * g


_IN_SPLITS = (0, SHIFT_WIDTH, SHIFT_WIDTH + RWKV_WIDTH, SHIFT_WIDTH + RWKV_WIDTH + S5_WIDTH,
              SHIFT_WIDTH + RWKV_WIDTH + 2 * S5_WIDTH, IN_WIDTH)


def _inproj_kernel(x_ref, g_ref, w_ref, zs_ref, ga_ref, u_ref, gb_ref, m_ref):
    h = _rms_norm(x_ref[...], g_ref[...]).astype(BF16)
    outs = (zs_ref, ga_ref, u_ref, gb_ref, m_ref)
    for o_ref, lo, hi in zip(outs, _IN_SPLITS[:-1], _IN_SPLITS[1:]):
        o_ref[...] = jnp.dot(h, w_ref[:, lo:hi], preferred_element_type=F32)


def _inproj(x2, norm_g, w_in_bf16, tm):
    t = x2.shape[0]
    widths = [hi - lo for lo, hi in zip(_IN_SPLITS[:-1], _IN_SPLITS[1:])]
    const = lambda i: (0, 0)
    return pl.pallas_call(
        _inproj_kernel,
        grid=(t // tm,),
        in_specs=[pl.BlockSpec((tm, D_MODEL), lambda i: (i, 0)),
                  pl.BlockSpec((1, D_MODEL), const),
                  pl.BlockSpec((D_MODEL, IN_WIDTH), const, pipeline_mode=pl.Buffered(1))],
        out_specs=[pl.BlockSpec((tm, w), lambda i: (i, 0)) for w in widths],
        out_shape=[jax.ShapeDtypeStruct((t, w), F32) for w in widths],
        compiler_params=pltpu.CompilerParams(dimension_semantics=("arbitrary",),
                                             vmem_limit_bytes=VMEM_LIMIT_BYTES),
        name="inproj",
    )(x2, norm_g, w_in_bf16)


def _wkv_kernel(zs_ref, mu_ref, w0_ref, wup_ref, a0_ref, aup_ref, kk_ref, ka_ref, rk_ref,
                lng_ref, lnb_ref, y_ref,
                prev_ref, h_ref, rt_ref, at_ref, bt_ref, kt_ref, bh_ref, kh_ref, v_ref, gc_ref, yc_ref,
                *, tt, p_small, p_chunk, p_state):
    n_chunks = tt // CHUNK

    @pl.when(pl.program_id(1) == 0)
    def _():
        prev_ref[...] = jnp.zeros_like(prev_ref)
        h_ref[...] = jnp.zeros_like(h_ref)

    z = zs_ref[...]
    row = _iota(z.shape, 0)
    z_prev = jnp.where(row == 0, prev_ref[...], pltpu.roll(z, 1, 0))
    prev_ref[...] = z[tt - 1:tt, :]
    zs = z + mu_ref[...] * (z_prev - z)
    r = zs[:, 0:RWKV_WIDTH]
    k = zs[:, RWKV_WIDTH:2 * RWKV_WIDTH]
    v = zs[:, 2 * RWKV_WIDTH:3 * RWKV_WIDTH]
    xw = zs[:, 3 * RWKV_WIDTH:3 * RWKV_WIDTH + LORA]
    xa = zs[:, 3 * RWKV_WIDTH + LORA:SHIFT_WIDTH]

    wl = w0_ref[...] + _mm(jnp.tanh(xw), wup_ref[...], p_small)
    logw = -math.exp(-0.5) * jax.nn.sigmoid(wl)
    iclr = jax.nn.sigmoid(a0_ref[...] + _mm(xa, aup_ref[...], p_small))

    seg = (_iota((SEG_TILE, SEG_TILE), 0) // HEAD_DIM) == (_iota((SEG_TILE, SEG_TILE), 1) // HEAD_DIM)
    ones_bd = jnp.where(seg, 1.0, 0.0).astype(BF16)

    kk = k * kk_ref[...]
    kk = kk * lax.rsqrt(jnp.maximum(_seg_sum(kk * kk, ones_bd), 1e-24))
    k_h = k * (1.0 + (iclr - 1.0) * ka_ref[...])
    a = -kk
    b = kk * iclr

    ti = _iota((tt, tt), 0)
    si = _iota((tt, tt), 1)
    tril_bd = jnp.where(((ti // CHUNK) == (si // CHUNK)) & (si <= ti), 1.0, 0.0).astype(BF16)
    lw_hi, lw_lo = _split_bf16(logw)
    cum = (jnp.dot(tril_bd, lw_hi, preferred_element_type=F32)
           + jnp.dot(tril_bd, lw_lo, preferred_element_type=F32))
    g_incl = jnp.exp(cum)
    g_excl = jnp.exp(cum - logw)
    g_inv = jnp.exp(-cum)
    by_chunk = lambda t: t.reshape(n_chunks, CHUNK, RWKV_WIDTH)
    g_c = jnp.exp(by_chunk(cum)[:, CHUNK - 1:CHUNK, :])

    rt_ref[...] = r * g_incl
    at_ref[...] = a * g_excl
    b_t = b * g_inv
    k_t = k_h * g_inv
    bt_ref[...] = b_t
    kt_ref[...] = k_t
    bh_ref[...] = (by_chunk(b_t) * g_c).reshape(tt, RWKV_WIDTH)
    kh_ref[...] = (by_chunk(k_t) * g_c).reshape(tt, RWKV_WIDTH)
    v_ref[...] = _swap_halves(v)
    gc_ref[...] = jnp.broadcast_to(g_c, (n_chunks, 8, RWKV_WIDTH)).reshape(n_chunks * 8, RWKV_WIDTH)

    gi = _iota((2 * CHUNK, 2 * CHUNK), 0)
    gj = _iota((2 * CHUNK, 2 * CHUNK), 1) % CHUNK
    g_mask = gj <= jnp.where(gi < CHUNK, gi, gi - (CHUNK + 1))
    lane = _iota((CHUNK, PAIR), 1)
    low = lane < HEAD_DIM
    own = (low, jnp.logical_not(low))
    diag = tuple(lane == _iota((CHUNK, PAIR), 0) + s * HEAD_DIM for s in range(2))
    zeros_cp = jnp.zeros((CHUNK, PAIR), F32)
    cast = (lambda t: t.astype(BF16)) if p_chunk == 1 else (lambda t: t)

    def chunk_body(c, carry):
        rows = pl.ds(pl.multiple_of(c * CHUNK, CHUNK), CHUNK)
        for grp in range(HEADS // HEAD_GROUP):
            heads = range(grp * HEAD_GROUP, (grp + 1) * HEAD_GROUP)
            pairs = range(grp * HEAD_GROUP // 2, (grp + 1) * HEAD_GROUP // 2)
            ld = lambda ref, p: ref[rows, p * PAIR:(p + 1) * PAIR]
            rt = {p: ld(rt_ref, p) for p in pairs}
            at = {p: ld(at_ref, p) for p in pairs}
            vs = {p: ld(v_ref, p) for p in pairs}
            bk = {p: cast(jnp.concatenate([ld(bt_ref, p), ld(kt_ref, p)], axis=0).T) for p in pairs}
            bkt = {p: cast(jnp.concatenate([ld(bh_ref, p), ld(kh_ref, p)], axis=0).T) for p in pairs}

            g, x, m, vo = {}, {}, {}, {}
            for h in heads:
                p, s = divmod(h, 2)
                lhs = jnp.concatenate([jnp.where(own[s], rt[p], 0.0), jnp.where(own[s], at[p], 0.0)], axis=0)
                g[h] = cast(jnp.where(g_mask, _mm(cast(lhs), bk[p], p_chunk), 0.0))
            for h in heads:
                p, s = divmod(h, 2)
                vo[h] = cast(jnp.where(own[s], 0.0, vs[p]))
                vak = _mm(g[h][CHUNK:, :], jnp.concatenate([cast(zeros_cp), vo[h]], axis=0), p_chunk)
                x[h] = jnp.where(own[s], at[p], 0.0) + vak
                m[h] = g[h][CHUNK:, :CHUNK]
            for step in range(5):
                for h in heads:
                    zz = _mm(m[h], jnp.concatenate([cast(x[h]), m[h]], axis=1), p_chunk)
                    x[h] = x[h] + zz[:, :PAIR]
                    m[h] = cast(zz[:, PAIR:])
            for h in heads:
                x[h] = x[h] + _mm(m[h], cast(x[h]), p_chunk)
            o9 = {}
            for h in heads:
                p, s = divmod(h, 2)
                lhs = jnp.concatenate([g[h][:CHUNK, :], bkt[p][s * CHUNK:(s + 1) * CHUNK, :]], axis=0)
                o9[h] = _mm(lhs, jnp.concatenate([cast(x[h]), vo[h]], axis=0), p_chunk)
            for p in pairs:
                t0, t1 = o9[2 * p][:CHUNK], o9[2 * p + 1][:CHUNK]
                b0, b1 = o9[2 * p][CHUNK:], o9[2 * p + 1][CHUNK:]
                gc = gc_ref[pl.ds(pl.multiple_of(c * 8, 8), 8), p * PAIR:(p + 1) * PAIR][0:1, :]
                r_p = rt[p] + jnp.where(low, t0, t1)
                y0 = jnp.where(low, t1, t0)
                phi = jnp.concatenate([jnp.where(low, jnp.where(diag[0], b0 + gc, b0), 0.0),
                                       jnp.where(low, 0.0, jnp.where(diag[1], b1 + gc, b1))], axis=0)
                hk = jnp.concatenate([jnp.where(low, 0.0, b0), jnp.where(low, b1, 0.0)], axis=0)
                zz = _mm(jnp.concatenate([r_p, phi], axis=0), h_ref[p], p_state)
                yc_ref[rows, p * PAIR:(p + 1) * PAIR] = zz[:CHUNK] + y0
                h_ref[p] = zz[CHUNK:] + hk
        return carry

    lax.fori_loop(0, n_chunks, chunk_body, 0)

    y = _swap_halves(yc_ref[...])
    mean = _seg_sum(y, ones_bd) * (1.0 / HEAD_DIM)
    yc = y - mean
    var = _seg_sum(yc * yc, ones_bd) * (1.0 / HEAD_DIM)
    yn = yc * lax.rsqrt(var + LNX_EPS) * lng_ref[...] + lnb_ref[...]
    bonus = _seg_sum(r * k_h * rk_ref[...], ones_bd)
    y_ref[...] = yn + bonus * v


def _wkv(zs, mu, w0, w_up, a0, a_up, k_k, k_a, r_k, lnx_g, lnx_b, bsz, seq, tt,
         p_small=1, p_chunk=1, p_state=3):
    n_t = seq // tt
    const = lambda b, i: (0, 0)
    row = lambda w: pl.BlockSpec((1, w), const)
    wide = pltpu.VMEM((tt, RWKV_WIDTH), F32)
    return pl.pallas_call(
        functools.partial(_wkv_kernel, tt=tt, p_small=p_small, p_chunk=p_chunk, p_state=p_state),
        grid=(bsz, n_t),
        in_specs=[pl.BlockSpec((tt, SHIFT_WIDTH), lambda b, i: (b * n_t + i, 0)),
                  row(SHIFT_WIDTH), row(RWKV_WIDTH),
                  pl.BlockSpec((LORA, RWKV_WIDTH), const),
                  row(RWKV_WIDTH),
                  pl.BlockSpec((LORA, RWKV_WIDTH), const),
                  row(RWKV_WIDTH), row(RWKV_WIDTH), row(RWKV_WIDTH), row(RWKV_WIDTH), row(RWKV_WIDTH)],
        out_specs=pl.BlockSpec((tt, RWKV_WIDTH), lambda b, i: (b * n_t + i, 0)),
        out_shape=jax.ShapeDtypeStruct((bsz * seq, RWKV_WIDTH), F32),
        scratch_shapes=[pltpu.VMEM((1, SHIFT_WIDTH), F32),
                        pltpu.VMEM((HEADS // 2, PAIR, PAIR), F32),
                        wide, wide, wide, wide, wide, wide, wide,
                        pltpu.VMEM((tt // CHUNK * 8, RWKV_WIDTH), F32), wide],
        compiler_params=pltpu.CompilerParams(dimension_semantics=("arbitrary", "arbitrary"),
                                             vmem_limit_bytes=VMEM_LIMIT_BYTES),
        name="wkv7",
    )(zs, mu, w0, w_up, a0, a_up, k_k, k_a, r_k, lnx_g, lnx_b)


def _s5_kernel(u_ref, bm_ref, cm_ref, pw_ref, d_ref, y_ref, carry_ref, st_ref, *, tt, p_in, p_out):
    @pl.when(pl.program_id(1) == 0)
    def _():
        carry_ref[...] = jnp.zeros_like(carry_ref)

    u = u_ref[...]
    hs = S5_HALF_STATE
    for half in range(2):
        st_ref[:, half * 2 * hs:(half + 1) * 2 * hs] = _mm(
            u[:, half * S5_HALF:(half + 1) * S5_HALF], bm_ref[half], p_in)

    row8 = _iota((SCAN_ROWS, SCAN_LANES), 0)
    n_blocks = tt // SCAN_ROWS
    for half in range(2):
        for q in range(hs // SCAN_LANES):
            re_l = pl.ds(half * 2 * hs + q * SCAN_LANES, SCAN_LANES)
            im_l = pl.ds(half * 2 * hs + hs + q * SCAN_LANES, SCAN_LANES)
            pw_re = pw_ref[:, re_l]
            pw_im = pw_ref[:, im_l]

            def block(i, carry, re_l=re_l, im_l=im_l, pw_re=pw_re, pw_im=pw_im):
                c_re, c_im = carry
                rows = pl.ds(pl.multiple_of(i * SCAN_ROWS, SCAN_ROWS), SCAN_ROWS)
                x_re = st_ref[rows, re_l]
                x_im = st_ref[rows, im_l]
                for s in (1, 2, 4):
                    l_re = pw_re[s - 1:s, :]
                    l_im = pw_im[s - 1:s, :]
                    s_re = jnp.where(row8 >= s, pltpu.roll(x_re, s, 0), 0.0)
                    s_im = jnp.where(row8 >= s, pltpu.roll(x_im, s, 0), 0.0)
                    x_re, x_im = (x_re + (l_re * s_re - l_im * s_im),
                                  x_im + (l_re * s_im + l_im * s_re))
                x_re, x_im = (x_re + (pw_re * c_re - pw_im * c_im),
                              x_im + (pw_re * c_im + pw_im * c_re))
                st_ref[rows, re_l] = x_re
                st_ref[rows, im_l] = x_im
                return x_re[SCAN_ROWS - 1:SCAN_ROWS, :], x_im[SCAN_ROWS - 1:SCAN_ROWS, :]

            c_re, c_im = lax.fori_loop(0, n_blocks, block, (carry_ref[0:1, re_l], carry_ref[0:1, im_l]))
            carry_ref[0:1, re_l] = c_re
            carry_ref[0:1, im_l] = c_im

    outs = []
    for half in range(2):
        outs.append(_mm(st_ref[:, half * 2 * hs:(half + 1) * 2 * hs], cm_ref[half], p_out))
    y_ref[...] = jnp.concatenate(outs, axis=1) + d_ref[...] * u


def _s5(u, bmat, cmat, pw, d_row, bsz, seq, tt, p_in=3, p_out=3):
    n_t = seq // tt
    hs = S5_HALF_STATE
    const2 = lambda b, i: (0, 0)
    const3 = lambda b, i: (0, 0, 0)
    return pl.pallas_call(
        functools.partial(_s5_kernel, tt=tt, p_in=p_in, p_out=p_out),
        grid=(bsz, n_t),
        in_specs=[pl.BlockSpec((tt, S5_WIDTH), lambda b, i: (b * n_t + i, 0)),
                  pl.BlockSpec((2, S5_HALF, 2 * hs), const3),
                  pl.BlockSpec((2, 2 * hs, S5_HALF), const3),
                  pl.BlockSpec((SCAN_ROWS, 4 * hs), const2),
                  pl.BlockSpec((1, S5_WIDTH), const2)],
        out_specs=pl.BlockSpec((tt, S5_WIDTH), lambda b, i: (b * n_t + i, 0)),
        out_shape=jax.ShapeDtypeStruct((bsz * seq, S5_WIDTH), F32),
        scratch_shapes=[pltpu.VMEM((SCAN_ROWS, 4 * hs), F32),
                        pltpu.VMEM((tt, 4 * hs), F32)],
        compiler_params=pltpu.CompilerParams(dimension_semantics=("arbitrary", "arbitrary"),
                                             vmem_limit_bytes=VMEM_LIMIT_BYTES),
        name="s5",
    )(u, bmat, cmat, pw, d_row)


def _s5_params(lam_re, lam_im, log_dt, b_re, b_im, c_re, c_im):
    dt = jnp.exp(log_dt)[:, None]
    mag = jnp.exp(lam_re * dt)
    ang = lam_im * dt
    ab_re = mag * jnp.cos(ang)
    ab_im = mag * jnp.sin(ang)
    den = lam_re * lam_re + lam_im * lam_im
    nr = ab_re - 1.0
    f_re = (nr * lam_re + ab_im * lam_im) / den
    f_im = (ab_im * lam_re - nr * lam_im) / den
    bb_re = f_re[..., None] * b_re - f_im[..., None] * b_im
    bb_im = f_re[..., None] * b_im + f_im[..., None] * b_re

    gh = S5_GROUPS // 2
    eye = jnp.eye(gh, dtype=F32)

    def in_half(bb):
        return jnp.einsum('gph,gk->ghkp', bb, eye).reshape(gh * S5_GROUP_DIM, gh * S5_STATE)

    def out_half(cc):
        return jnp.einsum('ghp,gk->gpkh', cc, eye).reshape(gh * S5_STATE, gh * S5_GROUP_DIM)

    bmat = jnp.stack([jnp.concatenate([in_half(bb_re[s]), in_half(bb_im[s])], axis=1)
                      for s in (slice(0, gh), slice(gh, None))])
    cmat = jnp.stack([jnp.concatenate([out_half(c_re[s]), -out_half(c_im[s])], axis=0)
                      for s in (slice(0, gh), slice(gh, None))])

    steps = jnp.arange(1, SCAN_ROWS + 1, dtype=F32)[:, None, None]
    pmag = jnp.exp(steps * (lam_re * dt)[None])
    pang = steps * ang[None]
    p_re = (pmag * jnp.cos(pang)).reshape(SCAN_ROWS, 2, gh * S5_STATE)
    p_im = (pmag * jnp.sin(pang)).reshape(SCAN_ROWS, 2, gh * S5_STATE)
    pw = jnp.concatenate([p_re[:, 0], p_im[:, 0], p_re[:, 1], p_im[:, 1]], axis=1)
    return bmat, cmat, pw


def _out_kernel(x_ref, ya_ref, ga_ref, ys_ref, gb_ref, m_ref, wglu_ref, bglu_ref, pa_ref, pb_ref,
                wout_ref, fg_ref, o_ref):
    y_a = ya_ref[...] * jax.nn.silu(ga_ref[...])
    proj_a = jnp.dot(y_a.astype(BF16), pa_ref[...], preferred_element_type=F32)
    ys = jax.nn.gelu(ys_ref[...])
    glu = jnp.dot(ys.astype(BF16), wglu_ref[...], preferred_element_type=F32) + bglu_ref[...]
    y_b = glu[:, :S5_WIDTH] * jax.nn.sigmoid(glu[:, S5_WIDTH:]) * jax.nn.silu(gb_ref[...])
    proj_b = jnp.dot(y_b.astype(BF16), pb_ref[...], preferred_element_type=F32)
    m = m_ref[...]
    merged = jax.nn.sigmoid(m[:, :D_MODEL]) * proj_a + jax.nn.sigmoid(m[:, D_MODEL:]) * proj_b
    x_new = x_ref[...] + jnp.dot(merged.astype(BF16), wout_ref[...], preferred_element_type=F32)
    o_ref[...] = _rms_norm(x_new, fg_ref[...])


def _out(x2, ya, ga, ys, gb, m, w_glu, b_glu, p_a, p_b, w_out, final_g, tm):
    t = x2.shape[0]
    const = lambda i: (0, 0)
    tile = lambda w: pl.BlockSpec((tm, w), lambda i: (i, 0))
    full = lambda a: pl.BlockSpec(a.shape, const)
    return pl.pallas_call(
        _out_kernel,
        grid=(t // tm,),
        in_specs=[tile(D_MODEL), tile(RWKV_WIDTH), tile(RWKV_WIDTH), tile(S5_WIDTH), tile(S5_WIDTH),
                  tile(2 * D_MODEL), full(w_glu), full(b_glu), full(p_a), full(p_b), full(w_out),
                  full(final_g)],
        out_specs=tile(D_MODEL),
        out_shape=jax.ShapeDtypeStruct((t, D_MODEL), F32),
        compiler_params=pltpu.CompilerParams(dimension_semantics=("arbitrary",),
                                             vmem_limit_bytes=VMEM_LIMIT_BYTES),
        name="merge_out",
    )(x2, ya, ga, ys, gb, m, w_glu, b_glu, p_a, p_b, w_out, final_g)


def kernel(x, norm_g, w_in, mu_shift, w0, w_up, a0, a_up, k_k, k_a, r_k, lnx_g, lnx_b, lam_re, lam_im,
           log_dt, b_re, b_im, c_re, c_im, d_skip, w_glu, b_glu, p_a, p_b, w_out, final_g):
    bsz, seq, _ = x.shape
    assert norm_g.shape[0] == 1, "single-layer block: the final RMSNorm is fused into the output kernel"
    tm = min(256, seq)
    tt_wkv = min(256, seq)
    tt_s5 = min(512, seq)
    x2 = x.reshape(bsz * seq, D_MODEL)
    row = lambda p: p.reshape(1, -1).astype(F32)
    l = 0
    zs, ga, u, gb, m = _inproj(x2, row(norm_g[l]), w_in[l].astype(BF16), tm)
    ya = _wkv(zs, row(mu_shift[l]), row(w0[l]), w_up[l], row(a0[l]), a_up[l], row(k_k[l]),
              row(k_a[l]), row(r_k[l]), row(lnx_g[l]), row(lnx_b[l]), bsz, seq, tt_wkv)
    bmat, cmat, pw = _s5_params(lam_re[l], lam_im[l], log_dt[l], b_re[l], b_im[l], c_re[l], c_im[l])
    ys = _s5(u, bmat, cmat, pw, row(d_skip[l]), bsz, seq, tt_s5)
    out = _out(x2, ya, ga, ys, gb, m, w_glu[l].astype(BF16), row(b_glu[l]), p_a[l].astype(BF16),
               p_b[l].astype(BF16), w_out[l].astype(BF16), row(final_g), tm)
    return out.reshape(bsz, seq, D_MODEL)
```

```python
import functools
import math

import jax
import jax.numpy as jnp
from jax import lax
from jax.experimental import pallas as pl
from jax.experimental.pallas import tpu as pltpu

F32 = jnp.float32
BF16 = jnp.bfloat16

D_MODEL = 1024
HEADS = 16
HEAD_DIM = 64
RWKV_WIDTH = HEADS * HEAD_DIM
LORA = 64
S5_GROUPS = 32
S5_GROUP_DIM = 16
S5_WIDTH = S5_GROUPS * S5_GROUP_DIM
S5_STATE = 64
SHIFT_WIDTH = 3 * RWKV_WIDTH + 2 * LORA
IN_WIDTH = SHIFT_WIDTH + RWKV_WIDTH + 2 * S5_WIDTH + 2 * D_MODEL
RMS_EPS = 1e-6
LNX_EPS = 64e-5

CHUNK = 64
PAIR = 2 * HEAD_DIM
HEAD_GROUP = 16
SEG_TILE = 256
S5_HALF = S5_WIDTH // 2
S5_HALF_STATE = (S5_GROUPS // 2) * S5_STATE
SCAN_ROWS = 8
S5_BATCH = 4

VMEM_LIMIT_BYTES = 56 * 1024 * 1024

NT_DIMS = (((1,), (1,)), ((), ()))
TN_DIMS = (((0,), (0,)), ((), ()))


def _split_bf16(x):
    hi = x.astype(BF16)
    lo = (x - hi.astype(F32)).astype(BF16)
    return hi, lo


def _mm(a, b, passes, dims=None):
    if dims is None:
        dims = (((a.ndim - 1,), (0,)), ((), ()))
    dg = functools.partial(lax.dot_general, dimension_numbers=dims, preferred_element_type=F32)
    if passes == 1:
        return dg(a.astype(BF16), b.astype(BF16))
    if passes == 3:
        ah, al = _split_bf16(a)
        bh, bl = _split_bf16(b)
        return dg(ah, bh) + (dg(ah, bl) + dg(al, bh))
    return dg(a, b, precision=lax.Precision.HIGHEST)


def _iota(shape, dim):
    return lax.broadcasted_iota(jnp.int32, shape, dim)


def _seg_sum(x, ones_bd):
    xb = x.astype(BF16)
    cols = [jnp.dot(xb[:, j * SEG_TILE:(j + 1) * SEG_TILE], ones_bd, preferred_element_type=F32)
            for j in range(x.shape[1] // SEG_TILE)]
    return jnp.concatenate(cols, axis=1)


def _swap_halves(x):
    n = x.shape[1]
    low = (_iota(x.shape, 1) % PAIR) < HEAD_DIM
    return jnp.where(low, pltpu.roll(x, n - HEAD_DIM, 1), pltpu.roll(x, HEAD_DIM, 1))


def _rms_norm(x, g):
    return x * lax.rsqrt(jnp.mean(x * x, axis=-1, keepdims=True) + RMS_EPS) * g


_IN_SPLITS = (0, SHIFT_WIDTH, SHIFT_WIDTH + RWKV_WIDTH, SHIFT_WIDTH + RWKV_WIDTH + S5_WIDTH,
              SHIFT_WIDTH + RWKV_WIDTH + 2 * S5_WIDTH, IN_WIDTH)


def _inproj_kernel(x_ref, g_ref, w_ref, zs_ref, ga_ref, u_ref, gb_ref, m_ref):
    h = _rms_norm(x_ref[...], g_ref[...]).astype(BF16)
    outs = (zs_ref, ga_ref, u_ref, gb_ref, m_ref)
    for o_ref, lo, hi in zip(outs, _IN_SPLITS[:-1], _IN_SPLITS[1:]):
        o_ref[...] = jnp.dot(h, w_ref[:, lo:hi], preferred_element_type=F32)


def _inproj(x2, norm_g, w_in_bf16, tm):
    t = x2.shape[0]
    widths = [hi - lo for lo, hi in zip(_IN_SPLITS[:-1], _IN_SPLITS[1:])]
    const = lambda i: (0, 0)
    return pl.pallas_call(
        _inproj_kernel,
        grid=(t // tm,),
        in_specs=[pl.BlockSpec((tm, D_MODEL), lambda i: (i, 0)),
                  pl.BlockSpec((1, D_MODEL), const),
                  pl.BlockSpec((D_MODEL, IN_WIDTH), const, pipeline_mode=pl.Buffered(1))],
        out_specs=[pl.BlockSpec((tm, w), lambda i: (i, 0)) for w in widths],
        out_shape=[jax.ShapeDtypeStruct((t, w), F32) for w in widths],
        compiler_params=pltpu.CompilerParams(dimension_semantics=("arbitrary",),
                                             vmem_limit_bytes=VMEM_LIMIT_BYTES),
        name="inproj",
    )(x2, norm_g, w_in_bf16)


def _wkv_kernel(zs_ref, mu_ref, w0_ref, wup_ref, a0_ref, aup_ref, kk_ref, ka_ref, rk_ref,
                lng_ref, lnb_ref, y_ref,
                prev_ref, h_ref, rt_ref, at_ref, bt_ref, kt_ref, bh_ref, kh_ref, v_ref, gc_ref, yc_ref,
                *, tt, p_small, p_chunk, p_state):
    n_chunks = tt // CHUNK

    @pl.when(pl.program_id(1) == 0)
    def _():
        prev_ref[...] = jnp.zeros_like(prev_ref)
        h_ref[...] = jnp.zeros_like(h_ref)

    z = zs_ref[...]
    row = _iota(z.shape, 0)
    z_prev = jnp.where(row == 0, prev_ref[...], pltpu.roll(z, 1, 0))
    prev_ref[...] = z[tt - 1:tt, :]
    zs = z + mu_ref[...] * (z_prev - z)
    r = zs[:, 0:RWKV_WIDTH]
    k = zs[:, RWKV_WIDTH:2 * RWKV_WIDTH]
    v = zs[:, 2 * RWKV_WIDTH:3 * RWKV_WIDTH]
    xw = zs[:, 3 * RWKV_WIDTH:3 * RWKV_WIDTH + LORA]
    xa = zs[:, 3 * RWKV_WIDTH + LORA:SHIFT_WIDTH]

    wl = w0_ref[...] + _mm(jnp.tanh(xw), wup_ref[...], p_small)
    logw = -math.exp(-0.5) * jax.nn.sigmoid(wl)
    iclr = jax.nn.sigmoid(a0_ref[...] + _mm(xa, aup_ref[...], p_small))

    seg = (_iota((SEG_TILE, SEG_TILE), 0) // HEAD_DIM) == (_iota((SEG_TILE, SEG_TILE), 1) // HEAD_DIM)
    ones_bd = jnp.where(seg, 1.0, 0.0).astype(BF16)

    kk = k * kk_ref[...]
    kk = kk * lax.rsqrt(jnp.maximum(_seg_sum(kk * kk, ones_bd), 1e-24))
    k_h = k * (1.0 + (iclr - 1.0) * ka_ref[...])
    a = -kk
    b = kk * iclr

    ti = _iota((tt, tt), 0)
    si = _iota((tt, tt), 1)
    tril_bd = jnp.where(((ti // CHUNK) == (si // CHUNK)) & (si <= ti), 1.0, 0.0).astype(BF16)
    lw_hi, lw_lo = _split_bf16(logw)
    cum = (jnp.dot(tril_bd, lw_hi, preferred_element_type=F32)
           + jnp.dot(tril_bd, lw_lo, preferred_element_type=F32))
    g_incl = jnp.exp(cum)
    g_excl = jnp.exp(cum - logw)
    g_inv = jnp.exp(-cum)
    by_chunk = lambda t: t.reshape(n_chunks, CHUNK, RWKV_WIDTH)
    g_c = jnp.exp(by_chunk(cum)[:, CHUNK - 1:CHUNK, :])

    rt_ref[...] = r * g_incl
    at_ref[...] = a * g_excl
    b_t = b * g_inv
    k_t = k_h * g_inv
    bt_ref[...] = b_t
    kt_ref[...] = k_t
    bh_ref[...] = (by_chunk(b_t) * g_c).reshape(tt, RWKV_WIDTH)
    kh_ref[...] = (by_chunk(k_t) * g_c).reshape(tt, RWKV_WIDTH)
    v_ref[...] = _swap_halves(v)
    gc_ref[...] = jnp.broadcast_to(g_c, (n_chunks, 8, RWKV_WIDTH)).reshape(n_chunks * 8, RWKV_WIDTH)

    gi = _iota((2 * CHUNK, 2 * CHUNK), 0)
    gj = _iota((2 * CHUNK, 2 * CHUNK), 1) % CHUNK
    g_mask = gj <= jnp.where(gi < CHUNK, gi, gi - (CHUNK + 1))
    lane = _iota((CHUNK, PAIR), 1)
    low = lane < HEAD_DIM
    own = (low, jnp.logical_not(low))
    diag = tuple(lane == _iota((CHUNK, PAIR), 0) + s * HEAD_DIM for s in range(2))
    zeros_cp = jnp.zeros((CHUNK, PAIR), F32)
    cast = (lambda t: t.astype(BF16)) if p_chunk == 1 else (lambda t: t)

    def chunk_body(c, carry):
        rows = pl.ds(pl.multiple_of(c * CHUNK, CHUNK), CHUNK)
        for grp in range(HEADS // HEAD_GROUP):
            heads = range(grp * HEAD_GROUP, (grp + 1) * HEAD_GROUP)
            pairs = range(grp * HEAD_GROUP // 2, (grp + 1) * HEAD_GROUP // 2)
            ld = lambda ref, p: ref[rows, p * PAIR:(p + 1) * PAIR]
            rt = {p: ld(rt_ref, p) for p in pairs}
            at = {p: ld(at_ref, p) for p in pairs}
            vs = {p: ld(v_ref, p) for p in pairs}
            bk = {p: cast(jnp.concatenate([ld(bt_ref, p), ld(kt_ref, p)], axis=0).T) for p in pairs}
            bkt = {p: cast(jnp.concatenate([ld(bh_ref, p), ld(kh_ref, p)], axis=0).T) for p in pairs}

            g, x, m, vo = {}, {}, {}, {}
            for h in heads:
                p, s = divmod(h, 2)
                lhs = jnp.concatenate([jnp.where(own[s], rt[p], 0.0), jnp.where(own[s], at[p], 0.0)], axis=0)
                g[h] = cast(jnp.where(g_mask, _mm(cast(lhs), bk[p], p_chunk), 0.0))
            for h in heads:
                p, s = divmod(h, 2)
                vo[h] = cast(jnp.where(own[s], 0.0, vs[p]))
                vak = _mm(g[h][CHUNK:, :], jnp.concatenate([cast(zeros_cp), vo[h]], axis=0), p_chunk)
                x[h] = jnp.where(own[s], at[p], 0.0) + vak
                m[h] = g[h][CHUNK:, :CHUNK]
            for step in range(5):
                for h in heads:
                    zz = _mm(m[h], jnp.concatenate([cast(x[h]), m[h]], axis=1), p_chunk)
                    x[h] = x[h] + zz[:, :PAIR]
                    m[h] = cast(zz[:, PAIR:])
            for h in heads:
                x[h] = x[h] + _mm(m[h], cast(x[h]), p_chunk)
            o9 = {}
            for h in heads:
                p, s = divmod(h, 2)
                lhs = jnp.concatenate([g[h][:CHUNK, :], bkt[p][s * CHUNK:(s + 1) * CHUNK, :]], axis=0)
                o9[h] = _mm(lhs, jnp.concatenate([cast(x[h]), vo[h]], axis=0), p_chunk)
            for p in pairs:
                t0, t1 = o9[2 * p][:CHUNK], o9[2 * p + 1][:CHUNK]
                b0, b1 = o9[2 * p][CHUNK:], o9[2 * p + 1][CHUNK:]
                gc = gc_ref[pl.ds(pl.multiple_of(c * 8, 8), 8), p * PAIR:(p + 1) * PAIR][0:1, :]
                r_p = rt[p] + jnp.where(low, t0, t1)
                y0 = jnp.where(low, t1, t0)
                phi = jnp.concatenate([jnp.where(low, jnp.where(diag[0], b0 + gc, b0), 0.0),
                                       jnp.where(low, 0.0, jnp.where(diag[1], b1 + gc, b1))], axis=0)
                hk = jnp.concatenate([jnp.where(low, 0.0, b0), jnp.where(low, b1, 0.0)], axis=0)
                zz = _mm(jnp.concatenate([r_p, phi], axis=0), h_ref[p], p_state)
                yc_ref[rows, p * PAIR:(p + 1) * PAIR] = zz[:CHUNK] + y0
                h_ref[p] = zz[CHUNK:] + hk
        return carry

    lax.fori_loop(0, n_chunks, chunk_body, 0)

    y = _swap_halves(yc_ref[...])
    mean = _seg_sum(y, ones_bd) * (1.0 / HEAD_DIM)
    yc = y - mean
    var = _seg_sum(yc * yc, ones_bd) * (1.0 / HEAD_DIM)
    yn = yc * lax.rsqrt(var + LNX_EPS) * lng_ref[...] + lnb_ref[...]
    bonus = _seg_sum(r * k_h * rk_ref[...], ones_bd)
    y_ref[...] = yn + bonus * v


def _wkv(zs, mu, w0, w_up, a0, a_up, k_k, k_a, r_k, lnx_g, lnx_b, bsz, seq, tt,
         p_small=1, p_chunk=1, p_state=3):
    n_t = seq // tt
    const = lambda b, i: (0, 0)
    row = lambda w: pl.BlockSpec((1, w), const)
    wide = pltpu.VMEM((tt, RWKV_WIDTH), F32)
    return pl.pallas_call(
        functools.partial(_wkv_kernel, tt=tt, p_small=p_small, p_chunk=p_chunk, p_state=p_state),
        grid=(bsz, n_t),
        in_specs=[pl.BlockSpec((tt, SHIFT_WIDTH), lambda b, i: (b * n_t + i, 0)),
                  row(SHIFT_WIDTH), row(RWKV_WIDTH),
                  pl.BlockSpec((LORA, RWKV_WIDTH), const),
                  row(RWKV_WIDTH),
                  pl.BlockSpec((LORA, RWKV_WIDTH), const),
                  row(RWKV_WIDTH), row(RWKV_WIDTH), row(RWKV_WIDTH), row(RWKV_WIDTH), row(RWKV_WIDTH)],
        out_specs=pl.BlockSpec((tt, RWKV_WIDTH), lambda b, i: (b * n_t + i, 0)),
        out_shape=jax.ShapeDtypeStruct((bsz * seq, RWKV_WIDTH), F32),
        scratch_shapes=[pltpu.VMEM((1, SHIFT_WIDTH), F32),
                        pltpu.VMEM((HEADS // 2, PAIR, PAIR), F32),
                        wide, wide, wide, wide, wide, wide, wide,
                        pltpu.VMEM((tt // CHUNK * 8, RWKV_WIDTH), F32), wide],
        compiler_params=pltpu.CompilerParams(dimension_semantics=("arbitrary", "arbitrary"),
                                             vmem_limit_bytes=VMEM_LIMIT_BYTES),
        name="wkv7",
    )(zs, mu, w0, w_up, a0, a_up, k_k, k_a, r_k, lnx_g, lnx_b)


def _s5_kernel(u_ref, w_ref, cm_ref, lam2_ref, d_ref, y_ref, prev_ref, carry_ref, st_ref, *, rows):
    @pl.when(pl.program_id(0) == 0)
    def _():
        prev_ref[...] = jnp.zeros_like(prev_ref)
        carry_ref[...] = jnp.zeros_like(carry_ref)

    u = u_ref[...]
    hs = S5_HALF_STATE
    shifted = pltpu.roll(u, S5_BATCH, 0)
    row8 = _iota((SCAN_ROWS, S5_WIDTH), 0)
    head = jnp.where(row8 < S5_BATCH, pltpu.roll(prev_ref[...], S5_BATCH, 0), shifted[:SCAN_ROWS])
    u_prev = jnp.concatenate([head, shifted[SCAN_ROWS:]], axis=0)
    prev_ref[...] = u[rows - SCAN_ROWS:, :]
    ub = u.astype(BF16)
    upb = u_prev.astype(BF16)
    for half in range(2):
        cols = slice(half * S5_HALF, (half + 1) * S5_HALF)
        st_ref[:, half * 2 * hs:(half + 1) * 2 * hs] = jnp.dot(
            jnp.concatenate([ub[:, cols], upb[:, cols]], axis=1), w_ref[half], preferred_element_type=F32)

    n_blocks = rows // SCAN_ROWS
    for half in range(2):
        re_l = pl.ds(half * 2 * hs, hs)
        im_l = pl.ds(half * 2 * hs + hs, hs)
        l_re = lam2_ref[:, re_l]
        l_im = lam2_ref[:, im_l]

        def block(i, carry, re_l=re_l, im_l=im_l, l_re=l_re, l_im=l_im):
            c_re, c_im = carry
            blk = pl.ds(pl.multiple_of(i * SCAN_ROWS, SCAN_ROWS), SCAN_ROWS)
            x_re = st_ref[blk, re_l] + (l_re * c_re - l_im * c_im)
            x_im = st_ref[blk, im_l] + (l_re * c_im + l_im * c_re)
            st_ref[blk, re_l] = x_re
            st_ref[blk, im_l] = x_im
            return x_re, x_im

        c_re, c_im = lax.fori_loop(0, n_blocks, block, (carry_ref[:, re_l], carry_ref[:, im_l]), unroll=2)
        carry_ref[:, re_l] = c_re
        carry_ref[:, im_l] = c_im

    outs = [jnp.dot(st_ref[:, half * 2 * hs:(half + 1) * 2 * hs].astype(BF16), cm_ref[half],
                    preferred_element_type=F32) for half in range(2)]
    y_ref[...] = jnp.concatenate(outs, axis=1) + d_ref[...] * u


def _s5(u_tb, wcat, cmat, lam2, d_row, rows):
    n_rows = u_tb.shape[0]
    hs = S5_HALF_STATE
    const2 = lambda i: (0, 0)
    const3 = lambda i: (0, 0, 0)
    return pl.pallas_call(
        functools.partial(_s5_kernel, rows=rows),
        grid=(n_rows // rows,),
        in_specs=[pl.BlockSpec((rows, S5_WIDTH), lambda i: (i, 0)),
                  pl.BlockSpec((2, 2 * S5_HALF, 2 * hs), const3),
                  pl.BlockSpec((2, 2 * hs, S5_HALF), const3),
                  pl.BlockSpec((SCAN_ROWS, 4 * hs), const2),
                  pl.BlockSpec((1, S5_WIDTH), const2)],
        out_specs=pl.BlockSpec((rows, S5_WIDTH), lambda i: (i, 0)),
        out_shape=jax.ShapeDtypeStruct((n_rows, S5_WIDTH), F32),
        scratch_shapes=[pltpu.VMEM((SCAN_ROWS, S5_WIDTH), F32),
                        pltpu.VMEM((SCAN_ROWS, 4 * hs), F32),
                        pltpu.VMEM((rows, 4 * hs), F32)],
        compiler_params=pltpu.CompilerParams(dimension_semantics=("arbitrary",),
                                             vmem_limit_bytes=VMEM_LIMIT_BYTES),
        name="s5",
    )(u_tb, wcat, cmat, lam2, d_row)


def _s5_params(lam_re, lam_im, log_dt, b_re, b_im, c_re, c_im):
    dt = jnp.exp(log_dt)[:, None]
    mag = jnp.exp(lam_re * dt)
    ang = lam_im * dt
    ab_re = mag * jnp.cos(ang)
    ab_im = mag * jnp.sin(ang)
    den = lam_re * lam_re + lam_im * lam_im
    nr = ab_re - 1.0
    f_re = (nr * lam_re + ab_im * lam_im) / den
    f_im = (ab_im * lam_re - nr * lam_im) / den
    bb_re = f_re[..., None] * b_re - f_im[..., None] * b_im
    bb_im = f_re[..., None] * b_im + f_im[..., None] * b_re

    gh = S5_GROUPS // 2
    eye = jnp.eye(gh, dtype=F32)

    def in_half(bb):
        return jnp.einsum('gph,gk->ghkp', bb, eye).reshape(gh * S5_GROUP_DIM, gh * S5_STATE)

    def out_half(cc):
        return jnp.einsum('ghp,gk->gpkh', cc, eye).reshape(gh * S5_STATE, gh * S5_GROUP_DIM)

    lb_re = ab_re[..., None] * bb_re - ab_im[..., None] * bb_im
    lb_im = ab_re[..., None] * bb_im + ab_im[..., None] * bb_re
    halves = (slice(0, gh), slice(gh, None))
    wcat = jnp.stack([jnp.concatenate(
        [jnp.concatenate([in_half(bb_re[s]), in_half(bb_im[s])], axis=1),
         jnp.concatenate([in_half(lb_re[s]), in_half(lb_im[s])], axis=1)], axis=0)
        for s in halves])
    cmat = jnp.stack([jnp.concatenate([out_half(c_re[s]), -out_half(c_im[s])], axis=0)
                      for s in halves])

    l2_re = (mag * mag * jnp.cos(2.0 * ang)).reshape(2, gh * S5_STATE)
    l2_im = (mag * mag * jnp.sin(2.0 * ang)).reshape(2, gh * S5_STATE)
    lam2 = jnp.concatenate([l2_re[0], l2_im[0], l2_re[1], l2_im[1]])[None, :]
    lam2 = jnp.broadcast_to(lam2, (SCAN_ROWS, 4 * gh * S5_STATE))
    return wcat.astype(BF16), cmat.astype(BF16), lam2


def _out_kernel(x_ref, ya_ref, ga_ref, ys_ref, gb_ref, m_ref, wglu_ref, bglu_ref, pa_ref, pb_ref,
                wout_ref, fg_ref, o_ref):
    y_a = ya_ref[...] * jax.nn.silu(ga_ref[...])
    proj_a = jnp.dot(y_a.astype(BF16), pa_ref[...], preferred_element_type=F32)
    ys = jax.nn.gelu(ys_ref[...])
    glu = jnp.dot(ys.astype(BF16), wglu_ref[...], preferred_element_type=F32) + bglu_ref[...]
    y_b = glu[:, :S5_WIDTH] * jax.nn.sigmoid(glu[:, S5_WIDTH:]) * jax.nn.silu(gb_ref[...])
    proj_b = jnp.dot(y_b.astype(BF16), pb_ref[...], preferred_element_type=F32)
    m = m_ref[...]
    merged = jax.nn.sigmoid(m[:, :D_MODEL]) * proj_a + jax.nn.sigmoid(m[:, D_MODEL:]) * proj_b
    x_new = x_ref[...] + jnp.dot(merged.astype(BF16), wout_ref[...], preferred_element_type=F32)
    o_ref[...] = _rms_norm(x_new, fg_ref[...])


def _out(x2, ya, ga, ys, gb, m, w_glu, b_glu, p_a, p_b, w_out, final_g, tm):
    t = x2.shape[0]
    const = lambda i: (0, 0)
    tile = lambda w: pl.BlockSpec((tm, w), lambda i: (i, 0))
    full = lambda a: pl.BlockSpec(a.shape, const)
    return pl.pallas_call(
        _out_kernel,
        grid=(t // tm,),
        in_specs=[tile(D_MODEL), tile(RWKV_WIDTH), tile(RWKV_WIDTH), tile(S5_WIDTH), tile(S5_WIDTH),
                  tile(2 * D_MODEL), full(w_glu), full(b_glu), full(p_a), full(p_b), full(w_out),
                  full(final_g)],
        out_specs=tile(D_MODEL),
        out_shape=jax.ShapeDtypeStruct((t, D_MODEL), F32),
        compiler_params=pltpu.CompilerParams(dimension_semantics=("arbitrary",),
                                             vmem_limit_bytes=VMEM_LIMIT_BYTES),
        name="merge_out",
    )(x2, ya, ga, ys, gb, m, w_glu, b_glu, p_a, p_b, w_out, final_g)


def kernel(x, norm_g, w_in, mu_shift, w0, w_up, a0, a_up, k_k, k_a, r_k, lnx_g, lnx_b, lam_re, lam_im,
           log_dt, b_re, b_im, c_re, c_im, d_skip, w_glu, b_glu, p_a, p_b, w_out, final_g):
    bsz, seq, _ = x.shape
    assert norm_g.shape[0] == 1, "single-layer block: the final RMSNorm is fused into the output kernel"
    tm = min(256, seq)
    tt_wkv = min(256, seq)
    assert bsz == S5_BATCH, "the S5 kernel packs 2 frames x 4 batch rows per sublane block"
    rows_s5 = min(1024, seq * bsz)
    x2 = x.reshape(bsz * seq, D_MODEL)
    row = lambda p: p.reshape(1, -1).astype(F32)
    l = 0
    zs, ga, u, gb, m = _inproj(x2, row(norm_g[l]), w_in[l].astype(BF16), tm)
    ya = _wkv(zs, row(mu_shift[l]), row(w0[l]), w_up[l], row(a0[l]), a_up[l], row(k_k[l]),
              row(k_a[l]), row(r_k[l]), row(lnx_g[l]), row(lnx_b[l]), bsz, seq, tt_wkv)
    wcat, cmat, lam2 = _s5_params(lam_re[l], lam_im[l], log_dt[l], b_re[l], b_im[l], c_re[l], c_im[l])
    frame_major = lambda t: t.reshape(bsz, seq, S5_WIDTH).transpose(1, 0, 2).reshape(seq * bsz, S5_WIDTH)
    ys_tb = _s5(frame_major(u), wcat, cmat, lam2, row(d_skip[l]), rows_s5)
    ys = ys_tb.reshape(seq, bsz, S5_WIDTH).transpose(1, 0, 2).reshape(bsz * seq, S5_WIDTH)
    out = _out(x2, ya, ga, ys, gb, m, w_glu[l].astype(BF16), row(b_glu[l]), p_a[l].astype(BF16),
               p_b[l].astype(BF16), w_out[l].astype(BF16), row(final_g), tm)
    return out.reshape(bsz, seq, D_MODEL)
```

```python
import functools
import math

import jax
import jax.numpy as jnp
from jax import lax
from jax.experimental import pallas as pl
from jax.experimental.pallas import tpu as pltpu

F32 = jnp.float32
BF16 = jnp.bfloat16

D_MODEL = 1024
HEADS = 16
HEAD_DIM = 64
RWKV_WIDTH = HEADS * HEAD_DIM
LORA = 64
S5_GROUPS = 32
S5_GROUP_DIM = 16
S5_WIDTH = S5_GROUPS * S5_GROUP_DIM
S5_STATE = 64
SHIFT_WIDTH = 3 * RWKV_WIDTH + 2 * LORA
IN_WIDTH = SHIFT_WIDTH + RWKV_WIDTH + 2 * S5_WIDTH + 2 * D_MODEL
RMS_EPS = 1e-6
LNX_EPS = 64e-5

LANES = 128
CHUNK = 64
PAIR = 2 * HEAD_DIM
HEAD_GROUP = 16
SEG_TILE = 256
S5_HALF = S5_WIDTH // 2
S5_HALF_STATE = (S5_GROUPS // 2) * S5_STATE
SCAN_ROWS = 8
S5_BATCH = 4

VMEM_LIMIT_BYTES = 56 * 1024 * 1024

NT_DIMS = (((1,), (1,)), ((), ()))
TN_DIMS = (((0,), (0,)), ((), ()))


def _split_bf16(x):
    hi = x.astype(BF16)
    lo = (x - hi.astype(F32)).astype(BF16)
    return hi, lo


def _mm(a, b, passes, dims=None):
    if dims is None:
        dims = (((a.ndim - 1,), (0,)), ((), ()))
    dg = functools.partial(lax.dot_general, dimension_numbers=dims, preferred_element_type=F32)
    if passes == 1:
        return dg(a.astype(BF16), b.astype(BF16))
    if passes == 3:
        ah, al = _split_bf16(a)
        bh, bl = _split_bf16(b)
        return dg(ah, bh) + (dg(ah, bl) + dg(al, bh))
    return dg(a, b, precision=lax.Precision.HIGHEST)


def _iota(shape, dim):
    return lax.broadcasted_iota(jnp.int32, shape, dim)


def _seg_sum(x, ones_bd):
    xb = x.astype(BF16)
    cols = [jnp.dot(xb[:, j * SEG_TILE:(j + 1) * SEG_TILE], ones_bd, preferred_element_type=F32)
            for j in range(x.shape[1] // SEG_TILE)]
    return jnp.concatenate(cols, axis=1)


def _swap_halves(x):
    n = x.shape[1]
    low = (_iota(x.shape, 1) % PAIR) < HEAD_DIM
    return jnp.where(low, pltpu.roll(x, n - HEAD_DIM, 1), pltpu.roll(x, HEAD_DIM, 1))


def _rms_norm(x, g):
    return x * lax.rsqrt(jnp.mean(x * x, axis=-1, keepdims=True) + RMS_EPS) * g


_IN_SPLITS = (0, SHIFT_WIDTH, SHIFT_WIDTH + RWKV_WIDTH, SHIFT_WIDTH + RWKV_WIDTH + S5_WIDTH,
              SHIFT_WIDTH + RWKV_WIDTH + 2 * S5_WIDTH, IN_WIDTH)


def _inproj_kernel(x_ref, g_ref, w_ref, zs_ref, ga_ref, u_ref, gb_ref, m_ref):
    h = _rms_norm(x_ref[...], g_ref[...]).astype(BF16)
    outs = (zs_ref, ga_ref, u_ref, gb_ref, m_ref)
    for o_ref, lo, hi in zip(outs, _IN_SPLITS[:-1], _IN_SPLITS[1:]):
        o_ref[...] = jnp.dot(h, w_ref[:, lo:hi], preferred_element_type=F32)


def _frame_batch_spec(tm, n_t):
    return pl.BlockSpec((tm, S5_WIDTH), lambda i: (i % n_t, i // n_t))


def _inproj(x2, norm_g, w_in_bf16, tm, bsz, seq):
    t = x2.shape[0]
    widths = [hi - lo for lo, hi in zip(_IN_SPLITS[:-1], _IN_SPLITS[1:])]
    const = lambda i: (0, 0)
    out_specs = [pl.BlockSpec((tm, w), lambda i: (i, 0)) for w in widths]
    out_shape = [jax.ShapeDtypeStruct((t, w), F32) for w in widths]
    out_specs[2] = _frame_batch_spec(tm, seq // tm)
    out_shape[2] = jax.ShapeDtypeStruct((seq, bsz * S5_WIDTH), F32)
    return pl.pallas_call(
        _inproj_kernel,
        grid=(t // tm,),
        in_specs=[pl.BlockSpec((tm, D_MODEL), lambda i: (i, 0)),
                  pl.BlockSpec((1, D_MODEL), const),
                  pl.BlockSpec((D_MODEL, IN_WIDTH), const, pipeline_mode=pl.Buffered(1))],
        out_specs=out_specs,
        out_shape=out_shape,
        compiler_params=pltpu.CompilerParams(dimension_semantics=("arbitrary",),
                                             vmem_limit_bytes=VMEM_LIMIT_BYTES),
        name="inproj",
    )(x2, norm_g, w_in_bf16)


def _wkv_kernel(zs_ref, mu_ref, w0_ref, wup_ref, a0_ref, aup_ref, kk_ref, ka_ref, rk_ref,
                lng_ref, lnb_ref, y_ref,
                prev_ref, h_ref, rt_ref, at_ref, bt_ref, kt_ref, bh_ref, kh_ref, v_ref, gc_ref, yc_ref,
                *, tt, p_small, p_chunk, p_state):
    n_chunks = tt // CHUNK

    @pl.when(pl.program_id(1) == 0)
    def _():
        prev_ref[...] = jnp.zeros_like(prev_ref)
        h_ref[...] = jnp.zeros_like(h_ref)

    z = zs_ref[...]
    row = _iota(z.shape, 0)
    z_prev = jnp.where(row == 0, prev_ref[...], pltpu.roll(z, 1, 0))
    prev_ref[...] = z[tt - 1:tt, :]
    zs = z + mu_ref[...] * (z_prev - z)
    r = zs[:, 0:RWKV_WIDTH]
    k = zs[:, RWKV_WIDTH:2 * RWKV_WIDTH]
    v = zs[:, 2 * RWKV_WIDTH:3 * RWKV_WIDTH]
    xw = zs[:, 3 * RWKV_WIDTH:3 * RWKV_WIDTH + LORA]
    xa = zs[:, 3 * RWKV_WIDTH + LORA:SHIFT_WIDTH]

    wl = w0_ref[...] + _mm(jnp.tanh(xw), wup_ref[...], p_small)
    logw = -math.exp(-0.5) * jax.nn.sigmoid(wl)
    iclr = jax.nn.sigmoid(a0_ref[...] + _mm(xa, aup_ref[...], p_small))

    seg = (_iota((SEG_TILE, SEG_TILE), 0) // HEAD_DIM) == (_iota((SEG_TILE, SEG_TILE), 1) // HEAD_DIM)
    ones_bd = jnp.where(seg, 1.0, 0.0).astype(BF16)

    kk = k * kk_ref[...]
    kk = kk * lax.rsqrt(jnp.maximum(_seg_sum(kk * kk, ones_bd), 1e-24))
    k_h = k * (1.0 + (iclr - 1.0) * ka_ref[...])
    a = -kk
    b = kk * iclr

    ti = _iota((tt, tt), 0)
    si = _iota((tt, tt), 1)
    tril_bd = jnp.where(((ti // CHUNK) == (si // CHUNK)) & (si <= ti), 1.0, 0.0).astype(BF16)
    lw_hi, lw_lo = _split_bf16(logw)
    cum = (jnp.dot(tril_bd, lw_hi, preferred_element_type=F32)
           + jnp.dot(tril_bd, lw_lo, preferred_element_type=F32))
    g_incl = jnp.exp(cum)
    g_excl = jnp.exp(cum - logw)
    g_inv = jnp.exp(-cum)
    by_chunk = lambda t: t.reshape(n_chunks, CHUNK, RWKV_WIDTH)
    g_c = jnp.exp(by_chunk(cum)[:, CHUNK - 1:CHUNK, :])

    rt_ref[...] = r * g_incl
    at_ref[...] = a * g_excl
    b_t = b * g_inv
    k_t = k_h * g_inv
    bt_ref[...] = b_t
    kt_ref[...] = k_t
    bh_ref[...] = (by_chunk(b_t) * g_c).reshape(tt, RWKV_WIDTH)
    kh_ref[...] = (by_chunk(k_t) * g_c).reshape(tt, RWKV_WIDTH)
    v_ref[...] = _swap_halves(v)
    gc_ref[...] = jnp.broadcast_to(g_c, (n_chunks, 8, RWKV_WIDTH)).reshape(n_chunks * 8, RWKV_WIDTH)

    gi = _iota((2 * CHUNK, 2 * CHUNK), 0)
    gj = _iota((2 * CHUNK, 2 * CHUNK), 1) % CHUNK
    g_mask = gj <= jnp.where(gi < CHUNK, gi, gi - (CHUNK + 1))
    lane = _iota((CHUNK, PAIR), 1)
    low = lane < HEAD_DIM
    own = (low, jnp.logical_not(low))
    zeros_cp = jnp.zeros((CHUNK, PAIR), F32)
    cast = (lambda t: t.astype(BF16)) if p_chunk == 1 else (lambda t: t)

    def chunk_body(c, carry):
        rows = pl.ds(pl.multiple_of(c * CHUNK, CHUNK), CHUNK)
        for grp in range(HEADS // HEAD_GROUP):
            heads = range(grp * HEAD_GROUP, (grp + 1) * HEAD_GROUP)
            pairs = range(grp * HEAD_GROUP // 2, (grp + 1) * HEAD_GROUP // 2)
            ld = lambda ref, p: ref[rows, p * PAIR:(p + 1) * PAIR]
            rt = {p: ld(rt_ref, p) for p in pairs}
            at = {p: ld(at_ref, p) for p in pairs}
            vs = {p: ld(v_ref, p) for p in pairs}
            bk = {p: cast(jnp.concatenate([ld(bt_ref, p), ld(kt_ref, p)], axis=0).T) for p in pairs}
            bkt = {p: cast(jnp.concatenate([ld(bh_ref, p), ld(kh_ref, p)], axis=0).T) for p in pairs}

            g, x, m, vo = {}, {}, {}, {}
            for h in heads:
                p, s = divmod(h, 2)
                lhs = jnp.concatenate([jnp.where(own[s], rt[p], 0.0), jnp.where(own[s], at[p], 0.0)], axis=0)
                g[h] = cast(jnp.where(g_mask, _mm(cast(lhs), bk[p], p_chunk), 0.0))
            for h in heads:
                p, s = divmod(h, 2)
                vo[h] = cast(jnp.where(own[s], 0.0, vs[p]))
                vak = _mm(g[h][CHUNK:, :], jnp.concatenate([cast(zeros_cp), vo[h]], axis=0), p_chunk)
                x[h] = jnp.where(own[s], at[p], 0.0) + vak
                m[h] = g[h][CHUNK:, :CHUNK]
            for step in range(5):
                for h in heads:
                    zz = _mm(m[h], jnp.concatenate([cast(x[h]), m[h]], axis=1), p_chunk)
                    x[h] = x[h] + zz[:, :PAIR]
                    m[h] = cast(zz[:, PAIR:])
            for h in heads:
                x[h] = x[h] + _mm(m[h], cast(x[h]), p_chunk)
            o9 = {}
            for h in heads:
                p, s = divmod(h, 2)
                lhs = jnp.concatenate([g[h][:CHUNK, :], bkt[p][s * CHUNK:(s + 1) * CHUNK, :]], axis=0)
                o9[h] = _mm(lhs, jnp.concatenate([cast(x[h]), vo[h]], axis=0), p_chunk)
            for p in pairs:
                t0, t1 = o9[2 * p][:CHUNK], o9[2 * p + 1][:CHUNK]
                b0, b1 = o9[2 * p][CHUNK:], o9[2 * p + 1][CHUNK:]
                gc = gc_ref[pl.ds(pl.multiple_of(c * 8, 8), 8), p * PAIR:(p + 1) * PAIR][0:1, :]
                g_col = jnp.broadcast_to(gc, (PAIR, PAIR)).T
                r_p = rt[p] + jnp.where(low, t0, t1)
                y0 = jnp.where(low, t1, t0)
                phi = jnp.concatenate([jnp.where(low, b0, 0.0), jnp.where(low, 0.0, b1)], axis=0)
                hk = jnp.concatenate([jnp.where(low, 0.0, b0), jnp.where(low, b1, 0.0)], axis=0)
                h0 = h_ref[p]
                zz = _mm(jnp.concatenate([r_p, phi], axis=0), h0, p_state)
                yc_ref[rows, p * PAIR:(p + 1) * PAIR] = zz[:CHUNK] + y0
                h_ref[p] = g_col * h0 + (zz[CHUNK:] + hk)
        return carry

    lax.fori_loop(0, n_chunks, chunk_body, 0)

    y = _swap_halves(yc_ref[...])
    mean = _seg_sum(y, ones_bd) * (1.0 / HEAD_DIM)
    yc = y - mean
    var = _seg_sum(yc * yc, ones_bd) * (1.0 / HEAD_DIM)
    yn = yc * lax.rsqrt(var + LNX_EPS) * lng_ref[...] + lnb_ref[...]
    bonus = _seg_sum(r * k_h * rk_ref[...], ones_bd)
    y_ref[...] = yn + bonus * v


def _wkv(zs, mu, w0, w_up, a0, a_up, k_k, k_a, r_k, lnx_g, lnx_b, bsz, seq, tt,
         p_small=1, p_chunk=1, p_state=1):
    n_t = seq // tt
    const = lambda b, i: (0, 0)
    row = lambda w: pl.BlockSpec((1, w), const)
    wide = pltpu.VMEM((tt, RWKV_WIDTH), F32)
    return pl.pallas_call(
        functools.partial(_wkv_kernel, tt=tt, p_small=p_small, p_chunk=p_chunk, p_state=p_state),
        grid=(bsz, n_t),
        in_specs=[pl.BlockSpec((tt, SHIFT_WIDTH), lambda b, i: (b * n_t + i, 0)),
                  row(SHIFT_WIDTH), row(RWKV_WIDTH),
                  pl.BlockSpec((LORA, RWKV_WIDTH), const),
                  row(RWKV_WIDTH),
                  pl.BlockSpec((LORA, RWKV_WIDTH), const),
                  row(RWKV_WIDTH), row(RWKV_WIDTH), row(RWKV_WIDTH), row(RWKV_WIDTH), row(RWKV_WIDTH)],
        out_specs=pl.BlockSpec((tt, RWKV_WIDTH), lambda b, i: (b * n_t + i, 0)),
        out_shape=jax.ShapeDtypeStruct((bsz * seq, RWKV_WIDTH), F32),
        scratch_shapes=[pltpu.VMEM((1, SHIFT_WIDTH), F32),
                        pltpu.VMEM((HEADS // 2, PAIR, PAIR), F32),
                        wide, wide, wide, wide, wide, wide, wide,
                        pltpu.VMEM((tt // CHUNK * 8, RWKV_WIDTH), F32), wide],
        compiler_params=pltpu.CompilerParams(dimension_semantics=("arbitrary", "arbitrary"),
                                             vmem_limit_bytes=VMEM_LIMIT_BYTES),
        name="wkv7",
    )(zs, mu, w0, w_up, a0, a_up, k_k, k_a, r_k, lnx_g, lnx_b)


def _s5_kernel(u_ref, w_ref, cm_ref, lam2_ref, d_ref, y_ref, ut_ref, prev_ref, carry_ref, st_ref, *, rows):
    @pl.when(pl.program_id(0) == 0)
    def _():
        prev_ref[...] = jnp.zeros_like(prev_ref)
        carry_ref[...] = jnp.zeros_like(carry_ref)

    frames = rows // S5_BATCH
    n_col = S5_WIDTH // LANES
    for b in range(S5_BATCH):
        for j in range(n_col):
            col = b * S5_WIDTH + j * LANES
            ut_ref[j, pl.ds(b, frames, stride=S5_BATCH), :] = u_ref[:, col:col + LANES]
    u = jnp.concatenate([ut_ref[j] for j in range(n_col)], axis=1)
    hs = S5_HALF_STATE
    shifted = pltpu.roll(u, S5_BATCH, 0)
    row8 = _iota((SCAN_ROWS, S5_WIDTH), 0)
    head = jnp.where(row8 < S5_BATCH, pltpu.roll(prev_ref[...], S5_BATCH, 0), shifted[:SCAN_ROWS])
    u_prev = jnp.concatenate([head, shifted[SCAN_ROWS:]], axis=0)
    prev_ref[...] = u[rows - SCAN_ROWS:, :]
    ub = u.astype(BF16)
    upb = u_prev.astype(BF16)
    for half in range(2):
        cols = slice(half * S5_HALF, (half + 1) * S5_HALF)
        st_ref[:, half * 2 * hs:(half + 1) * 2 * hs] = jnp.dot(
            jnp.concatenate([ub[:, cols], upb[:, cols]], axis=1), w_ref[half], preferred_element_type=F32)

    n_blocks = rows // SCAN_ROWS
    for half in range(2):
        re_l = pl.ds(half * 2 * hs, hs)
        im_l = pl.ds(half * 2 * hs + hs, hs)
        l_re = lam2_ref[:, re_l]
        l_im = lam2_ref[:, im_l]

        def block(i, carry, re_l=re_l, im_l=im_l, l_re=l_re, l_im=l_im):
            c_re, c_im = carry
            blk = pl.ds(pl.multiple_of(i * SCAN_ROWS, SCAN_ROWS), SCAN_ROWS)
            x_re = st_ref[blk, re_l] + (l_re * c_re - l_im * c_im)
            x_im = st_ref[blk, im_l] + (l_re * c_im + l_im * c_re)
            st_ref[blk, re_l] = x_re
            st_ref[blk, im_l] = x_im
            return x_re, x_im

        c_re, c_im = lax.fori_loop(0, n_blocks, block, (carry_ref[:, re_l], carry_ref[:, im_l]), unroll=2)
        carry_ref[:, re_l] = c_re
        carry_ref[:, im_l] = c_im

    outs = [jnp.dot(st_ref[:, half * 2 * hs:(half + 1) * 2 * hs].astype(BF16), cm_ref[half],
                    preferred_element_type=F32) for half in range(2)]
    y = jnp.concatenate(outs, axis=1) + d_ref[...] * u
    for j in range(n_col):
        ut_ref[j] = y[:, j * LANES:(j + 1) * LANES]
    for b in range(S5_BATCH):
        for j in range(n_col):
            col = b * S5_WIDTH + j * LANES
            y_ref[:, col:col + LANES] = ut_ref[j, pl.ds(b, frames, stride=S5_BATCH), :]


def _s5(u_lb, wcat, cmat, lam2, d_row, rows):
    seq = u_lb.shape[0]
    frames = rows // S5_BATCH
    hs = S5_HALF_STATE
    const2 = lambda i: (0, 0)
    const3 = lambda i: (0, 0, 0)
    return pl.pallas_call(
        functools.partial(_s5_kernel, rows=rows),
        grid=(seq // frames,),
        in_specs=[pl.BlockSpec((frames, S5_BATCH * S5_WIDTH), lambda i: (i, 0)),
                  pl.BlockSpec((2, 2 * S5_HALF, 2 * hs), const3),
                  pl.BlockSpec((2, 2 * hs, S5_HALF), const3),
                  pl.BlockSpec((SCAN_ROWS, 4 * hs), const2),
                  pl.BlockSpec((1, S5_WIDTH), const2)],
        out_specs=pl.BlockSpec((frames, S5_BATCH * S5_WIDTH), lambda i: (i, 0)),
        out_shape=jax.ShapeDtypeStruct((seq, S5_BATCH * S5_WIDTH), F32),
        scratch_shapes=[pltpu.VMEM((S5_WIDTH // LANES, rows, LANES), F32),
                        pltpu.VMEM((SCAN_ROWS, S5_WIDTH), F32),
                        pltpu.VMEM((SCAN_ROWS, 4 * hs), F32),
                        pltpu.VMEM((rows, 4 * hs), F32)],
        compiler_params=pltpu.CompilerParams(dimension_semantics=("arbitrary",),
                                             vmem_limit_bytes=VMEM_LIMIT_BYTES),
        name="s5",
    )(u_lb, wcat, cmat, lam2, d_row)


def _s5_params(lam_re, lam_im, log_dt, b_re, b_im, c_re, c_im):
    dt = jnp.exp(log_dt)[:, None]
    mag = jnp.exp(lam_re * dt)
    ang = lam_im * dt
    ab_re = mag * jnp.cos(ang)
    ab_im = mag * jnp.sin(ang)
    den = lam_re * lam_re + lam_im * lam_im
    nr = ab_re - 1.0
    f_re = (nr * lam_re + ab_im * lam_im) / den
    f_im = (ab_im * lam_re - nr * lam_im) / den
    bb_re = f_re[..., None] * b_re - f_im[..., None] * b_im
    bb_im = f_re[..., None] * b_im + f_im[..., None] * b_re

    gh = S5_GROUPS // 2
    eye = jnp.eye(gh, dtype=F32)

    def in_half(bb):
        return jnp.einsum('gph,gk->ghkp', bb, eye).reshape(gh * S5_GROUP_DIM, gh * S5_STATE)

    def out_half(cc):
        return jnp.einsum('ghp,gk->gpkh', cc, eye).reshape(gh * S5_STATE, gh * S5_GROUP_DIM)

    lb_re = ab_re[..., None] * bb_re - ab_im[..., None] * bb_im
    lb_im = ab_re[..., None] * bb_im + ab_im[..., None] * bb_re
    halves = (slice(0, gh), slice(gh, None))
    wcat = jnp.stack([jnp.concatenate(
        [jnp.concatenate([in_half(bb_re[s]), in_half(bb_im[s])], axis=1),
         jnp.concatenate([in_half(lb_re[s]), in_half(lb_im[s])], axis=1)], axis=0)
        for s in halves])
    cmat = jnp.stack([jnp.concatenate([out_half(c_re[s]), -out_half(c_im[s])], axis=0)
                      for s in halves])

    l2_re = (mag * mag * jnp.cos(2.0 * ang)).reshape(2, gh * S5_STATE)
    l2_im = (mag * mag * jnp.sin(2.0 * ang)).reshape(2, gh * S5_STATE)
    lam2 = jnp.concatenate([l2_re[0], l2_im[0], l2_re[1], l2_im[1]])[None, :]
    lam2 = jnp.broadcast_to(lam2, (SCAN_ROWS, 4 * gh * S5_STATE))
    return wcat.astype(BF16), cmat.astype(BF16), lam2


def _out_kernel(x_ref, ya_ref, ga_ref, ys_ref, gb_ref, m_ref, wglu_ref, bglu_ref, pa_ref, pb_ref,
                wout_ref, fg_ref, o_ref):
    y_a = ya_ref[...] * jax.nn.silu(ga_ref[...])
    proj_a = jnp.dot(y_a.astype(BF16), pa_ref[...], preferred_element_type=F32)
    ys = jax.nn.gelu(ys_ref[...])
    glu = jnp.dot(ys.astype(BF16), wglu_ref[...], preferred_element_type=F32) + bglu_ref[...]
    y_b = glu[:, :S5_WIDTH] * jax.nn.sigmoid(glu[:, S5_WIDTH:]) * jax.nn.silu(gb_ref[...])
    proj_b = jnp.dot(y_b.astype(BF16), pb_ref[...], preferred_element_type=F32)
    m = m_ref[...]
    merged = jax.nn.sigmoid(m[:, :D_MODEL]) * proj_a + jax.nn.sigmoid(m[:, D_MODEL:]) * proj_b
    x_new = x_ref[...] + jnp.dot(merged.astype(BF16), wout_ref[...], preferred_element_type=F32)
    o_ref[...] = _rms_norm(x_new, fg_ref[...])


def _out(x2, ya, ga, ys_lb, gb, m, w_glu, b_glu, p_a, p_b, w_out, final_g, tm):
    t = x2.shape[0]
    const = lambda i: (0, 0)
    tile = lambda w: pl.BlockSpec((tm, w), lambda i: (i, 0))
    full = lambda a: pl.BlockSpec(a.shape, const)
    return pl.pallas_call(
        _out_kernel,
        grid=(t // tm,),
        in_specs=[tile(D_MODEL), tile(RWKV_WIDTH), tile(RWKV_WIDTH),
                  _frame_batch_spec(tm, ys_lb.shape[0] // tm), tile(S5_WIDTH),
                  tile(2 * D_MODEL), full(w_glu), full(b_glu), full(p_a), full(p_b), full(w_out),
                  full(final_g)],
        out_specs=tile(D_MODEL),
        out_shape=jax.ShapeDtypeStruct((t, D_MODEL), F32),
        compiler_params=pltpu.CompilerParams(dimension_semantics=("arbitrary",),
                                             vmem_limit_bytes=VMEM_LIMIT_BYTES),
        name="merge_out",
    )(x2, ya, ga, ys_lb, gb, m, w_glu, b_glu, p_a, p_b, w_out, final_g)


def kernel(x, norm_g, w_in, mu_shift, w0, w_up, a0, a_up, k_k, k_a, r_k, lnx_g, lnx_b, lam_re, lam_im,
           log_dt, b_re, b_im, c_re, c_im, d_skip, w_glu, b_glu, p_a, p_b, w_out, final_g):
    bsz, seq, _ = x.shape
    assert norm_g.shape[0] == 1, "single-layer block: the final RMSNorm is fused into the output kernel"
    tm = min(256, seq)
    tt_wkv = min(256, seq)
    assert bsz == S5_BATCH, "the S5 kernel packs 2 frames x 4 batch rows per sublane block"
    rows_s5 = min(1024, seq * bsz)
    x2 = x.reshape(bsz * seq, D_MODEL)
    row = lambda p: p.reshape(1, -1).astype(F32)
    l = 0
    zs, ga, u_lb, gb, m = _inproj(x2, row(norm_g[l]), w_in[l].astype(BF16), tm, bsz, seq)
    ya = _wkv(zs, row(mu_shift[l]), row(w0[l]), w_up[l], row(a0[l]), a_up[l], row(k_k[l]),
              row(k_a[l]), row(r_k[l]), row(lnx_g[l]), row(lnx_b[l]), bsz, seq, tt_wkv)
    wcat, cmat, lam2 = _s5_params(lam_re[l], lam_im[l], log_dt[l], b_re[l], b_im[l], c_re[l], c_im[l])
    ys_lb = _s5(u_lb, wcat, cmat, lam2, row(d_skip[l]), rows_s5)
    out = _out(x2, ya, ga, ys_lb, gb, m, w_glu[l].astype(BF16), row(b_glu[l]), p_a[l].astype(BF16),
               p_b[l].astype(BF16), w_out[l].astype(BF16), row(final_g), tm)
    return out.reshape(bsz, seq, D_MODEL)
```

```python
import functools
import math

import jax
import jax.numpy as jnp
from jax import lax
from jax.experimental import pallas as pl
from jax.experimental.pallas import tpu as pltpu

F32 = jnp.float32
BF16 = jnp.bfloat16

D_MODEL = 1024
HEADS = 16
HEAD_DIM = 64
RWKV_WIDTH = HEADS * HEAD_DIM
LORA = 64
S5_GROUPS = 32
S5_GROUP_DIM = 16
S5_WIDTH = S5_GROUPS * S5_GROUP_DIM
S5_STATE = 64
SHIFT_WIDTH = 3 * RWKV_WIDTH + 2 * LORA
IN_WIDTH = SHIFT_WIDTH + RWKV_WIDTH + 2 * S5_WIDTH + 2 * D_MODEL
RMS_EPS = 1e-6
LNX_EPS = 64e-5

LANES = 128
CHUNK = 64
PAIR = 2 * HEAD_DIM
CHUNK_GROUP = 1
SEG_TILE = 256
S5_HALF = S5_WIDTH // 2
S5_HALF_STATE = (S5_GROUPS // 2) * S5_STATE
SCAN_ROWS = 8
S5_BATCH = 4

VMEM_LIMIT_BYTES = 56 * 1024 * 1024

NT_DIMS = (((1,), (1,)), ((), ()))
TN_DIMS = (((0,), (0,)), ((), ()))


def _split_bf16(x):
    hi = x.astype(BF16)
    lo = (x - hi.astype(F32)).astype(BF16)
    return hi, lo


def _mm(a, b, passes, dims=None):
    if dims is None:
        dims = (((a.ndim - 1,), (0,)), ((), ()))
    dg = functools.partial(lax.dot_general, dimension_numbers=dims, preferred_element_type=F32)
    if passes == 1:
        return dg(a.astype(BF16), b.astype(BF16))
    if passes == 3:
        ah, al = _split_bf16(a)
        bh, bl = _split_bf16(b)
        return dg(ah, bh) + (dg(ah, bl) + dg(al, bh))
    return dg(a, b, precision=lax.Precision.HIGHEST)


def _iota(shape, dim):
    return lax.broadcasted_iota(jnp.int32, shape, dim)


def _seg_sum(x, ones_bd):
    xb = x.astype(BF16)
    cols = [jnp.dot(xb[:, j * SEG_TILE:(j + 1) * SEG_TILE], ones_bd, preferred_element_type=F32)
            for j in range(x.shape[1] // SEG_TILE)]
    return jnp.concatenate(cols, axis=1)


def _swap_halves(x):
    n = x.shape[1]
    low = (_iota(x.shape, 1) % PAIR) < HEAD_DIM
    return jnp.where(low, pltpu.roll(x, n - HEAD_DIM, 1), pltpu.roll(x, HEAD_DIM, 1))


def _rms_norm(x, g):
    return x * lax.rsqrt(jnp.mean(x * x, axis=-1, keepdims=True) + RMS_EPS) * g


_IN_SPLITS = (0, SHIFT_WIDTH, SHIFT_WIDTH + RWKV_WIDTH, SHIFT_WIDTH + RWKV_WIDTH + S5_WIDTH,
              SHIFT_WIDTH + RWKV_WIDTH + 2 * S5_WIDTH, IN_WIDTH)


def _inproj_kernel(x_ref, g_ref, w_ref, zs_ref, ga_ref, u_ref, gb_ref, m_ref):
    h = _rms_norm(x_ref[...], g_ref[...]).astype(BF16)
    outs = (zs_ref, ga_ref, u_ref, gb_ref, m_ref)
    for o_ref, lo, hi in zip(outs, _IN_SPLITS[:-1], _IN_SPLITS[1:]):
        o_ref[...] = jnp.dot(h, w_ref[:, lo:hi], preferred_element_type=F32).astype(o_ref.dtype)


def _frame_batch_spec(tm, n_t):
    return pl.BlockSpec((tm, S5_WIDTH), lambda i: (i % n_t, i // n_t))


def _inproj(x2, norm_g, w_in_bf16, tm, bsz, seq):
    t = x2.shape[0]
    widths = [hi - lo for lo, hi in zip(_IN_SPLITS[:-1], _IN_SPLITS[1:])]
    const = lambda i: (0, 0)
    out_specs = [pl.BlockSpec((tm, w), lambda i: (i, 0)) for w in widths]
    out_shape = [jax.ShapeDtypeStruct((t, w), dt) for w, dt in zip(widths, (F32, BF16, F32, BF16, BF16))]
    out_specs[2] = _frame_batch_spec(tm, seq // tm)
    out_shape[2] = jax.ShapeDtypeStruct((seq, bsz * S5_WIDTH), F32)
    return pl.pallas_call(
        _inproj_kernel,
        grid=(t // tm,),
        in_specs=[pl.BlockSpec((tm, D_MODEL), lambda i: (i, 0)),
                  pl.BlockSpec((1, D_MODEL), const),
                  pl.BlockSpec((D_MODEL, IN_WIDTH), const, pipeline_mode=pl.Buffered(1))],
        out_specs=out_specs,
        out_shape=out_shape,
        compiler_params=pltpu.CompilerParams(dimension_semantics=("arbitrary",),
                                             vmem_limit_bytes=VMEM_LIMIT_BYTES),
        name="inproj",
    )(x2, norm_g, w_in_bf16)


def _wkv_kernel(zs_ref, mu_ref, w0_ref, wup_ref, a0_ref, aup_ref, kk_ref, ka_ref, rk_ref,
                lng_ref, lnb_ref, y_ref,
                prev_ref, h_ref, rt_ref, at_ref, bt_ref, kt_ref, bh_ref, kh_ref, v_ref, gc_ref, yc_ref,
                *, tt, p_small, p_chunk, p_state):
    n_chunks = tt // CHUNK

    @pl.when(pl.program_id(1) == 0)
    def _():
        prev_ref[...] = jnp.zeros_like(prev_ref)
        h_ref[...] = jnp.zeros_like(h_ref)

    z = zs_ref[...]
    row = _iota(z.shape, 0)
    z_prev = jnp.where(row == 0, prev_ref[...], pltpu.roll(z, 1, 0))
    prev_ref[...] = z[tt - 1:tt, :]
    zs = z + mu_ref[...] * (z_prev - z)
    r = zs[:, 0:RWKV_WIDTH]
    k = zs[:, RWKV_WIDTH:2 * RWKV_WIDTH]
    v = zs[:, 2 * RWKV_WIDTH:3 * RWKV_WIDTH]
    xw = zs[:, 3 * RWKV_WIDTH:3 * RWKV_WIDTH + LORA]
    xa = zs[:, 3 * RWKV_WIDTH + LORA:SHIFT_WIDTH]

    wl = w0_ref[...] + _mm(jnp.tanh(xw), wup_ref[...], p_small)
    logw = -math.exp(-0.5) * jax.nn.sigmoid(wl)
    iclr = jax.nn.sigmoid(a0_ref[...] + _mm(xa, aup_ref[...], p_small))

    seg = (_iota((SEG_TILE, SEG_TILE), 0) // HEAD_DIM) == (_iota((SEG_TILE, SEG_TILE), 1) // HEAD_DIM)
    ones_bd = jnp.where(seg, 1.0, 0.0).astype(BF16)

    kk = k * kk_ref[...]
    kk = kk * lax.rsqrt(jnp.maximum(_seg_sum(kk * kk, ones_bd), 1e-24))
    k_h = k * (1.0 + (iclr - 1.0) * ka_ref[...])
    a = -kk
    b = kk * iclr

    ti = _iota((tt, tt), 0)
    si = _iota((tt, tt), 1)
    tril_bd = jnp.where(((ti // CHUNK) == (si // CHUNK)) & (si <= ti), 1.0, 0.0).astype(BF16)
    lw_hi, lw_lo = _split_bf16(logw)
    cum = (jnp.dot(tril_bd, lw_hi, preferred_element_type=F32)
           + jnp.dot(tril_bd, lw_lo, preferred_element_type=F32))
    g_incl = jnp.exp(cum)
    g_excl = jnp.exp(cum - logw)
    g_inv = jnp.exp(-cum)
    by_chunk = lambda t: t.reshape(n_chunks, CHUNK, RWKV_WIDTH)
    g_c = jnp.exp(by_chunk(cum)[:, CHUNK - 1:CHUNK, :])

    rt_ref[...] = r * g_incl
    at_ref[...] = a * g_excl
    b_t = b * g_inv
    k_t = k_h * g_inv
    bt_ref[...] = b_t
    kt_ref[...] = k_t
    bh_ref[...] = (by_chunk(b_t) * g_c).reshape(tt, RWKV_WIDTH)
    kh_ref[...] = (by_chunk(k_t) * g_c).reshape(tt, RWKV_WIDTH)
    v_ref[...] = _swap_halves(v)
    gc_ref[...] = jnp.broadcast_to(g_c, (n_chunks, 8, RWKV_WIDTH)).reshape(n_chunks * 8, RWKV_WIDTH)

    gi = _iota((2 * CHUNK, 2 * CHUNK), 0)
    gj = _iota((2 * CHUNK, 2 * CHUNK), 1) % CHUNK
    g_mask = gj <= jnp.where(gi < CHUNK, gi, gi - (CHUNK + 1))
    lane = _iota((CHUNK, PAIR), 1)
    low = lane < HEAD_DIM
    own = (low, jnp.logical_not(low))
    zeros_cp = jnp.zeros((CHUNK, PAIR), F32)
    cast = (lambda t: t.astype(BF16)) if p_chunk == 1 else (lambda t: t)
    n_pairs = HEADS // 2

    def block(c, size):
        return pl.ds(c * size, size) if isinstance(c, int) else pl.ds(pl.multiple_of(c * size, size), size)

    def phase_a(c0):
        chunks = range(CHUNK_GROUP)
        cps = [(c, p) for c in chunks for p in range(n_pairs)]
        chs = [(c, h) for c in chunks for h in range(HEADS)]
        ld = lambda ref, c, p: ref[block(c0 + c, CHUNK), p * PAIR:(p + 1) * PAIR]
        rt = {(c, p): ld(rt_ref, c, p) for c, p in cps}
        at = {(c, p): ld(at_ref, c, p) for c, p in cps}
        vs = {(c, p): ld(v_ref, c, p) for c, p in cps}

        def key_major(first_ref, second_ref, c, p):
            return cast(jnp.concatenate([ld(first_ref, c, p), ld(second_ref, c, p)], axis=0).T)

        bk = {(c, p): key_major(bt_ref, kt_ref, c, p) for c, p in cps}
        bkt = {(c, p): key_major(bh_ref, kh_ref, c, p) for c, p in cps}

        g, x, vo, m = {}, {}, {}, {}
        for c, h in chs:
            p, s = divmod(h, 2)
            lhs = jnp.concatenate([jnp.where(own[s], rt[c, p], 0.0), jnp.where(own[s], at[c, p], 0.0)], axis=0)
            g[c, h] = cast(jnp.where(g_mask, _mm(cast(lhs), bk[c, p], p_chunk), 0.0))
        for c, h in chs:
            p, s = divmod(h, 2)
            vo[c, h] = cast(jnp.where(own[s], 0.0, vs[c, p]))
            vak = _mm(g[c, h][CHUNK:, :], jnp.concatenate([cast(zeros_cp), vo[c, h]], axis=0), p_chunk)
            x[c, h] = jnp.where(own[s], at[c, p], 0.0) + vak
            m[c, h] = g[c, h][CHUNK:, :CHUNK]
        for step in range(5):
            for ch in chs:
                zz = _mm(m[ch], jnp.concatenate([cast(x[ch]), m[ch]], axis=1), p_chunk)
                x[ch] = x[ch] + zz[:, :PAIR]
                m[ch] = cast(zz[:, PAIR:])
        for ch in chs:
            x[ch] = x[ch] + _mm(m[ch], cast(x[ch]), p_chunk)
        o9 = {}
        for c, h in chs:
            p, s = divmod(h, 2)
            lhs = jnp.concatenate([g[c, h][:CHUNK, :], bkt[c, p][s * CHUNK:(s + 1) * CHUNK, :]], axis=0)
            o9[c, h] = _mm(lhs, jnp.concatenate([cast(x[c, h]), vo[c, h]], axis=0), p_chunk)
        out = {}
        for c, p in cps:
            t0, t1 = o9[c, 2 * p][:CHUNK], o9[c, 2 * p + 1][:CHUNK]
            b0, b1 = o9[c, 2 * p][CHUNK:], o9[c, 2 * p + 1][CHUNK:]
            r_p = rt[c, p] + jnp.where(low, t0, t1)
            y0 = jnp.where(low, t1, t0)
            phi = jnp.concatenate([jnp.where(low, b0, 0.0), jnp.where(low, 0.0, b1)], axis=0)
            hk = jnp.concatenate([jnp.where(low, 0.0, b0), jnp.where(low, b1, 0.0)], axis=0)
            out[c, p] = (cast(jnp.concatenate([r_p, phi], axis=0)), y0, hk)
        return out

    def phase_b(c0, c, prepared, states):
        new_states = []
        for p in range(n_pairs):
            lhs, y0, hk = prepared[c, p]
            gc = gc_ref[block(c0 + c, 8), p * PAIR:(p + 1) * PAIR][0:1, :]
            g_col = jnp.broadcast_to(gc, (PAIR, PAIR)).T
            zz = _mm(lhs, states[p], p_state)
            yc_ref[block(c0 + c, CHUNK), p * PAIR:(p + 1) * PAIR] = zz[:CHUNK] + y0
            new_states.append(g_col * states[p] + (zz[CHUNK:] + hk))
        return new_states

    def group_body(gi, carry):
        c0 = gi * CHUNK_GROUP
        prepared = phase_a(c0)
        states = [h_ref[p] for p in range(n_pairs)]
        for c in range(CHUNK_GROUP):
            states = phase_b(c0, c, prepared, states)
        for p in range(n_pairs):
            h_ref[p] = states[p]
        return carry

    for gi in range(n_chunks // CHUNK_GROUP):
        group_body(gi, 0)

    y = _swap_halves(yc_ref[...])
    mean = _seg_sum(y, ones_bd) * (1.0 / HEAD_DIM)
    yc = y - mean
    var = _seg_sum(yc * yc, ones_bd) * (1.0 / HEAD_DIM)
    yn = yc * lax.rsqrt(var + LNX_EPS) * lng_ref[...] + lnb_ref[...]
    bonus = _seg_sum(r * k_h * rk_ref[...], ones_bd)
    y_ref[...] = yn + bonus * v


def _wkv(zs, mu, w0, w_up, a0, a_up, k_k, k_a, r_k, lnx_g, lnx_b, bsz, seq, tt,
         p_small=1, p_chunk=1, p_state=1):
    n_t = seq // tt
    const = lambda b, i: (0, 0)
    row = lambda w: pl.BlockSpec((1, w), const)
    wide = pltpu.VMEM((tt, RWKV_WIDTH), F32)
    return pl.pallas_call(
        functools.partial(_wkv_kernel, tt=tt, p_small=p_small, p_chunk=p_chunk, p_state=p_state),
        grid=(bsz, n_t),
        in_specs=[pl.BlockSpec((tt, SHIFT_WIDTH), lambda b, i: (b * n_t + i, 0)),
                  row(SHIFT_WIDTH), row(RWKV_WIDTH),
                  pl.BlockSpec((LORA, RWKV_WIDTH), const),
                  row(RWKV_WIDTH),
                  pl.BlockSpec((LORA, RWKV_WIDTH), const),
                  row(RWKV_WIDTH), row(RWKV_WIDTH), row(RWKV_WIDTH), row(RWKV_WIDTH), row(RWKV_WIDTH)],
        out_specs=pl.BlockSpec((tt, RWKV_WIDTH), lambda b, i: (b * n_t + i, 0)),
        out_shape=jax.ShapeDtypeStruct((bsz * seq, RWKV_WIDTH), F32),
        scratch_shapes=[pltpu.VMEM((1, SHIFT_WIDTH), F32),
                        pltpu.VMEM((HEADS // 2, PAIR, PAIR), F32),
                        wide, wide, wide, wide, wide, wide, wide,
                        pltpu.VMEM((tt // CHUNK * 8, RWKV_WIDTH), F32), wide],
        compiler_params=pltpu.CompilerParams(dimension_semantics=("arbitrary", "arbitrary"),
                                             vmem_limit_bytes=VMEM_LIMIT_BYTES),
        name="wkv7",
    )(zs, mu, w0, w_up, a0, a_up, k_k, k_a, r_k, lnx_g, lnx_b)


def _s5_kernel(u_ref, w_ref, cm_ref, lam2_ref, d_ref, y_ref, ut_ref, prev_ref, carry_ref, st_ref, *, rows):
    @pl.when(pl.program_id(0) == 0)
    def _():
        prev_ref[...] = jnp.zeros_like(prev_ref)
        carry_ref[...] = jnp.zeros_like(carry_ref)

    frames = rows // S5_BATCH
    n_col = S5_WIDTH // LANES
    for b in range(S5_BATCH):
        for j in range(n_col):
            col = b * S5_WIDTH + j * LANES
            ut_ref[j, pl.ds(b, frames, stride=S5_BATCH), :] = u_ref[:, col:col + LANES]
    u = jnp.concatenate([ut_ref[j] for j in range(n_col)], axis=1)
    hs = S5_HALF_STATE
    shifted = pltpu.roll(u, S5_BATCH, 0)
    row8 = _iota((SCAN_ROWS, S5_WIDTH), 0)
    head = jnp.where(row8 < S5_BATCH, pltpu.roll(prev_ref[...], S5_BATCH, 0), shifted[:SCAN_ROWS])
    u_prev = jnp.concatenate([head, shifted[SCAN_ROWS:]], axis=0)
    prev_ref[...] = u[rows - SCAN_ROWS:, :]
    ub = u.astype(BF16)
    upb = u_prev.astype(BF16)
    for half in range(2):
        cols = slice(half * S5_HALF, (half + 1) * S5_HALF)
        st_ref[:, half * 2 * hs:(half + 1) * 2 * hs] = jnp.dot(
            jnp.concatenate([ub[:, cols], upb[:, cols]], axis=1), w_ref[half], preferred_element_type=F32)

    n_blocks = rows // SCAN_ROWS
    for half in range(2):
        re_l = pl.ds(half * 2 * hs, hs)
        im_l = pl.ds(half * 2 * hs + hs, hs)
        l_re = lam2_ref[:, re_l]
        l_im = lam2_ref[:, im_l]

        def block(i, carry, re_l=re_l, im_l=im_l, l_re=l_re, l_im=l_im):
            c_re, c_im = carry
            blk = pl.ds(pl.multiple_of(i * SCAN_ROWS, SCAN_ROWS), SCAN_ROWS)
            x_re = st_ref[blk, re_l] + (l_re * c_re - l_im * c_im)
            x_im = st_ref[blk, im_l] + (l_re * c_im + l_im * c_re)
            st_ref[blk, re_l] = x_re
            st_ref[blk, im_l] = x_im
            return x_re, x_im

        c_re, c_im = lax.fori_loop(0, n_blocks, block, (carry_ref[:, re_l], carry_ref[:, im_l]), unroll=2)
        carry_ref[:, re_l] = c_re
        carry_ref[:, im_l] = c_im

    outs = [jnp.dot(st_ref[:, half * 2 * hs:(half + 1) * 2 * hs].astype(BF16), cm_ref[half],
                    preferred_element_type=F32) for half in range(2)]
    y = jnp.concatenate(outs, axis=1) + d_ref[...] * u
    for j in range(n_col):
        ut_ref[j] = y[:, j * LANES:(j + 1) * LANES]
    for b in range(S5_BATCH):
        for j in range(n_col):
            col = b * S5_WIDTH + j * LANES
            y_ref[:, col:col + LANES] = ut_ref[j, pl.ds(b, frames, stride=S5_BATCH), :]


def _s5(u_lb, wcat, cmat, lam2, d_row, rows):
    seq = u_lb.shape[0]
    frames = rows // S5_BATCH
    hs = S5_HALF_STATE
    const2 = lambda i: (0, 0)
    const3 = lambda i: (0, 0, 0)
    return pl.pallas_call(
        functools.partial(_s5_kernel, rows=rows),
        grid=(seq // frames,),
        in_specs=[pl.BlockSpec((frames, S5_BATCH * S5_WIDTH), lambda i: (i, 0)),
                  pl.BlockSpec((2, 2 * S5_HALF, 2 * hs), const3),
                  pl.BlockSpec((2, 2 * hs, S5_HALF), const3),
                  pl.BlockSpec((SCAN_ROWS, 4 * hs), const2),
                  pl.BlockSpec((1, S5_WIDTH), const2)],
        out_specs=pl.BlockSpec((frames, S5_BATCH * S5_WIDTH), lambda i: (i, 0)),
        out_shape=jax.ShapeDtypeStruct((seq, S5_BATCH * S5_WIDTH), F32),
        scratch_shapes=[pltpu.VMEM((S5_WIDTH // LANES, rows, LANES), F32),
                        pltpu.VMEM((SCAN_ROWS, S5_WIDTH), F32),
                        pltpu.VMEM((SCAN_ROWS, 4 * hs), F32),
                        pltpu.VMEM((rows, 4 * hs), F32)],
        compiler_params=pltpu.CompilerParams(dimension_semantics=("arbitrary",),
                                             vmem_limit_bytes=VMEM_LIMIT_BYTES),
        name="s5",
    )(u_lb, wcat, cmat, lam2, d_row)


def _s5_params(lam_re, lam_im, log_dt, b_re, b_im, c_re, c_im):
    dt = jnp.exp(log_dt)[:, None]
    mag = jnp.exp(lam_re * dt)
    ang = lam_im * dt
    ab_re = mag * jnp.cos(ang)
    ab_im = mag * jnp.sin(ang)
    den = lam_re * lam_re + lam_im * lam_im
    nr = ab_re - 1.0
    f_re = (nr * lam_re + ab_im * lam_im) / den
    f_im = (ab_im * lam_re - nr * lam_im) / den
    bb_re = f_re[..., None] * b_re - f_im[..., None] * b_im
    bb_im = f_re[..., None] * b_im + f_im[..., None] * b_re

    gh = S5_GROUPS // 2
    eye = jnp.eye(gh, dtype=F32)

    def in_half(bb):
        return jnp.einsum('gph,gk->ghkp', bb, eye).reshape(gh * S5_GROUP_DIM, gh * S5_STATE)

    def out_half(cc):
        return jnp.einsum('ghp,gk->gpkh', cc, eye).reshape(gh * S5_STATE, gh * S5_GROUP_DIM)

    lb_re = ab_re[..., None] * bb_re - ab_im[..., None] * bb_im
    lb_im = ab_re[..., None] * bb_im + ab_im[..., None] * bb_re
    halves = (slice(0, gh), slice(gh, None))
    wcat = jnp.stack([jnp.concatenate(
        [jnp.concatenate([in_half(bb_re[s]), in_half(bb_im[s])], axis=1),
         jnp.concatenate([in_half(lb_re[s]), in_half(lb_im[s])], axis=1)], axis=0)
        for s in halves])
    cmat = jnp.stack([jnp.concatenate([out_half(c_re[s]), -out_half(c_im[s])], axis=0)
                      for s in halves])

    l2_re = (mag * mag * jnp.cos(2.0 * ang)).reshape(2, gh * S5_STATE)
    l2_im = (mag * mag * jnp.sin(2.0 * ang)).reshape(2, gh * S5_STATE)
    lam2 = jnp.concatenate([l2_re[0], l2_im[0], l2_re[1], l2_im[1]])[None, :]
    lam2 = jnp.broadcast_to(lam2, (SCAN_ROWS, 4 * gh * S5_STATE))
    return wcat.astype(BF16), cmat.astype(BF16), lam2


def _out_kernel(x_ref, ya_ref, ga_ref, ys_ref, gb_ref, m_ref, wglu_ref, bglu_ref, pa_ref, pb_ref,
                wout_ref, fg_ref, o_ref):
    y_a = ya_ref[...] * jax.nn.silu(ga_ref[...].astype(F32))
    proj_a = jnp.dot(y_a.astype(BF16), pa_ref[...], preferred_element_type=F32)
    ys = jax.nn.gelu(ys_ref[...])
    glu = jnp.dot(ys.astype(BF16), wglu_ref[...], preferred_element_type=F32) + bglu_ref[...]
    y_b = glu[:, :S5_WIDTH] * jax.nn.sigmoid(glu[:, S5_WIDTH:]) * jax.nn.silu(gb_ref[...].astype(F32))
    proj_b = jnp.dot(y_b.astype(BF16), pb_ref[...], preferred_element_type=F32)
    m = m_ref[...].astype(F32)
    merged = jax.nn.sigmoid(m[:, :D_MODEL]) * proj_a + jax.nn.sigmoid(m[:, D_MODEL:]) * proj_b
    x_new = x_ref[...] + jnp.dot(merged.astype(BF16), wout_ref[...], preferred_element_type=F32)
    o_ref[...] = _rms_norm(x_new, fg_ref[...])


def _out(x2, ya, ga, ys_lb, gb, m, w_glu, b_glu, p_a, p_b, w_out, final_g, tm):
    t = x2.shape[0]
    const = lambda i: (0, 0)
    tile = lambda w: pl.BlockSpec((tm, w), lambda i: (i, 0))
    full = lambda a: pl.BlockSpec(a.shape, const)
    return pl.pallas_call(
        _out_kernel,
        grid=(t // tm,),
        in_specs=[tile(D_MODEL), tile(RWKV_WIDTH), tile(RWKV_WIDTH),
                  _frame_batch_spec(tm, ys_lb.shape[0] // tm), tile(S5_WIDTH),
                  tile(2 * D_MODEL), full(w_glu), full(b_glu), full(p_a), full(p_b), full(w_out),
                  full(final_g)],
        out_specs=tile(D_MODEL),
        out_shape=jax.ShapeDtypeStruct((t, D_MODEL), F32),
        compiler_params=pltpu.CompilerParams(dimension_semantics=("arbitrary",),
                                             vmem_limit_bytes=VMEM_LIMIT_BYTES),
        name="merge_out",
    )(x2, ya, ga, ys_lb, gb, m, w_glu, b_glu, p_a, p_b, w_out, final_g)


def kernel(x, norm_g, w_in, mu_shift, w0, w_up, a0, a_up, k_k, k_a, r_k, lnx_g, lnx_b, lam_re, lam_im,
           log_dt, b_re, b_im, c_re, c_im, d_skip, w_glu, b_glu, p_a, p_b, w_out, final_g):
    bsz, seq, _ = x.shape
    assert norm_g.shape[0] == 1, "single-layer block: the final RMSNorm is fused into the output kernel"
    tm = min(512, seq)
    tt_wkv = min(256, seq)
    assert bsz == S5_BATCH, "the S5 kernel packs 2 frames x 4 batch rows per sublane block"
    rows_s5 = min(1024, seq * bsz)
    x2 = x.reshape(bsz * seq, D_MODEL)
    row = lambda p: p.reshape(1, -1).astype(F32)
    l = 0
    zs, ga, u_lb, gb, m = _inproj(x2, row(norm_g[l]), w_in[l].astype(BF16), tm, bsz, seq)
    ya = _wkv(zs, row(mu_shift[l]), row(w0[l]), w_up[l], row(a0[l]), a_up[l], row(k_k[l]),
              row(k_a[l]), row(r_k[l]), row(lnx_g[l]), row(lnx_b[l]), bsz, seq, tt_wkv)
    wcat, cmat, lam2 = _s5_params(lam_re[l], lam_im[l], log_dt[l], b_re[l], b_im[l], c_re[l], c_im[l])
    ys_lb = _s5(u_lb, wcat, cmat, lam2, row(d_skip[l]), rows_s5)
    out = _out(x2, ya, ga, ys_lb, gb, m, w_glu[l].astype(BF16), row(b_glu[l]), p_a[l].astype(BF16),
               p_b[l].astype(BF16), w_out[l].astype(BF16), row(final_g), tm)
    return out.reshape(bsz, seq, D_MODEL)
```

```python
import functools
import math

import jax
import jax.numpy as jnp
from jax import lax
from jax.experimental import pallas as pl
from jax.experimental.pallas import tpu as pltpu

F32 = jnp.float32
BF16 = jnp.bfloat16

D_MODEL = 1024
HEADS = 16
HEAD_DIM = 64
RWKV_WIDTH = HEADS * HEAD_DIM
LORA = 64
S5_GROUPS = 32
S5_GROUP_DIM = 16
S5_WIDTH = S5_GROUPS * S5_GROUP_DIM
S5_STATE = 64
SHIFT_WIDTH = 3 * RWKV_WIDTH + 2 * LORA
IN_WIDTH = SHIFT_WIDTH + RWKV_WIDTH + 2 * S5_WIDTH + 2 * D_MODEL
RMS_EPS = 1e-6
LNX_EPS = 64e-5

LANES = 128
CHUNK = 64
PAIR = 2 * HEAD_DIM
CHUNK_GROUP = 1
SEG_TILE = 256
S5_PARTS = 4
S5_HALF = S5_WIDTH // S5_PARTS
S5_HALF_STATE = (S5_GROUPS // S5_PARTS) * S5_STATE
SCAN_ROWS = 8
S5_BATCH = 4

VMEM_LIMIT_BYTES = 56 * 1024 * 1024

NT_DIMS = (((1,), (1,)), ((), ()))
TN_DIMS = (((0,), (0,)), ((), ()))


def _split_bf16(x):
    hi = x.astype(BF16)
    lo = (x - hi.astype(F32)).astype(BF16)
    return hi, lo


def _mm(a, b, passes, dims=None):
    if dims is None:
        dims = (((a.ndim - 1,), (0,)), ((), ()))
    dg = functools.partial(lax.dot_general, dimension_numbers=dims, preferred_element_type=F32)
    if passes == 1:
        return dg(a.astype(BF16), b.astype(BF16))
    if passes == 3:
        ah, al = _split_bf16(a)
        bh, bl = _split_bf16(b)
        return dg(ah, bh) + (dg(ah, bl) + dg(al, bh))
    return dg(a, b, precision=lax.Precision.HIGHEST)


def _iota(shape, dim):
    return lax.broadcasted_iota(jnp.int32, shape, dim)


def _seg_sum(x, ones_bd):
    xb = x.astype(BF16)
    cols = [jnp.dot(xb[:, j * SEG_TILE:(j + 1) * SEG_TILE], ones_bd, preferred_element_type=F32)
            for j in range(x.shape[1] // SEG_TILE)]
    return jnp.concatenate(cols, axis=1)


def _swap_halves(x):
    n = x.shape[1]
    low = (_iota(x.shape, 1) % PAIR) < HEAD_DIM
    return jnp.where(low, pltpu.roll(x, n - HEAD_DIM, 1), pltpu.roll(x, HEAD_DIM, 1))


def _rms_norm(x, g):
    return x * lax.rsqrt(jnp.mean(x * x, axis=-1, keepdims=True) + RMS_EPS) * g


_IN_SPLITS = (0, SHIFT_WIDTH, SHIFT_WIDTH + RWKV_WIDTH, SHIFT_WIDTH + RWKV_WIDTH + S5_WIDTH,
              SHIFT_WIDTH + RWKV_WIDTH + 2 * S5_WIDTH, IN_WIDTH)


def _inproj_kernel(x_ref, g_ref, w_ref, zs_ref, ga_ref, u_ref, gb_ref, m_ref):
    h = _rms_norm(x_ref[...], g_ref[...]).astype(BF16)
    outs = (zs_ref, ga_ref, u_ref, gb_ref, m_ref)
    for o_ref, lo, hi in zip(outs, _IN_SPLITS[:-1], _IN_SPLITS[1:]):
        o_ref[...] = jnp.dot(h, w_ref[:, lo:hi], preferred_element_type=F32).astype(o_ref.dtype)


def _frame_batch_spec(tm, n_t):
    return pl.BlockSpec((tm, S5_WIDTH), lambda i: (i % n_t, i // n_t))


def _inproj(x2, norm_g, w_in_bf16, tm, bsz, seq):
    t = x2.shape[0]
    widths = [hi - lo for lo, hi in zip(_IN_SPLITS[:-1], _IN_SPLITS[1:])]
    const = lambda i: (0, 0)
    out_specs = [pl.BlockSpec((tm, w), lambda i: (i, 0)) for w in widths]
    out_shape = [jax.ShapeDtypeStruct((t, w), dt) for w, dt in zip(widths, (F32, BF16, F32, BF16, BF16))]
    out_specs[2] = _frame_batch_spec(tm, seq // tm)
    out_shape[2] = jax.ShapeDtypeStruct((seq, bsz * S5_WIDTH), F32)
    return pl.pallas_call(
        _inproj_kernel,
        grid=(t // tm,),
        in_specs=[pl.BlockSpec((tm, D_MODEL), lambda i: (i, 0)),
                  pl.BlockSpec((1, D_MODEL), const),
                  pl.BlockSpec((D_MODEL, IN_WIDTH), const, pipeline_mode=pl.Buffered(1))],
        out_specs=out_specs,
        out_shape=out_shape,
        compiler_params=pltpu.CompilerParams(dimension_semantics=("arbitrary",),
                                             vmem_limit_bytes=VMEM_LIMIT_BYTES),
        name="inproj",
    )(x2, norm_g, w_in_bf16)


def _wkv_kernel(zs_ref, mu_ref, w0_ref, wup_ref, a0_ref, aup_ref, kk_ref, ka_ref, rk_ref,
                lng_ref, lnb_ref, y_ref,
                prev_ref, h_ref, rt_ref, at_ref, bt_ref, kt_ref, bh_ref, kh_ref, v_ref, gc_ref, yc_ref,
                *, tt, p_small, p_chunk, p_state):
    n_chunks = tt // CHUNK

    @pl.when(pl.program_id(1) == 0)
    def _():
        prev_ref[...] = jnp.zeros_like(prev_ref)
        h_ref[...] = jnp.zeros_like(h_ref)

    z = zs_ref[...]
    row = _iota(z.shape, 0)
    z_prev = jnp.where(row == 0, prev_ref[...], pltpu.roll(z, 1, 0))
    prev_ref[...] = z[tt - 1:tt, :]
    zs = z + mu_ref[...] * (z_prev - z)
    r = zs[:, 0:RWKV_WIDTH]
    k = zs[:, RWKV_WIDTH:2 * RWKV_WIDTH]
    v = zs[:, 2 * RWKV_WIDTH:3 * RWKV_WIDTH]
    xw = zs[:, 3 * RWKV_WIDTH:3 * RWKV_WIDTH + LORA]
    xa = zs[:, 3 * RWKV_WIDTH + LORA:SHIFT_WIDTH]

    wl = w0_ref[...] + _mm(jnp.tanh(xw), wup_ref[...], p_small)
    logw = -math.exp(-0.5) * jax.nn.sigmoid(wl)
    iclr = jax.nn.sigmoid(a0_ref[...] + _mm(xa, aup_ref[...], p_small))

    seg = (_iota((SEG_TILE, SEG_TILE), 0) // HEAD_DIM) == (_iota((SEG_TILE, SEG_TILE), 1) // HEAD_DIM)
    ones_bd = jnp.where(seg, 1.0, 0.0).astype(BF16)

    kk = k * kk_ref[...]
    kk = kk * lax.rsqrt(jnp.maximum(_seg_sum(kk * kk, ones_bd), 1e-24))
    k_h = k * (1.0 + (iclr - 1.0) * ka_ref[...])
    a = -kk
    b = kk * iclr

    ti = _iota((tt, tt), 0)
    si = _iota((tt, tt), 1)
    tril_bd = jnp.where(((ti // CHUNK) == (si // CHUNK)) & (si <= ti), 1.0, 0.0).astype(BF16)
    lw_hi, lw_lo = _split_bf16(logw)
    cum = (jnp.dot(tril_bd, lw_hi, preferred_element_type=F32)
           + jnp.dot(tril_bd, lw_lo, preferred_element_type=F32))
    g_incl = jnp.exp(cum)
    g_excl = jnp.exp(cum - logw)
    g_inv = jnp.exp(-cum)
    by_chunk = lambda t: t.reshape(n_chunks, CHUNK, RWKV_WIDTH)
    g_c = jnp.exp(by_chunk(cum)[:, CHUNK - 1:CHUNK, :])

    rt_ref[...] = r * g_incl
    at_ref[...] = a * g_excl
    b_t = b * g_inv
    k_t = k_h * g_inv
    bt_ref[...] = b_t
    kt_ref[...] = k_t
    bh_ref[...] = (by_chunk(b_t) * g_c).reshape(tt, RWKV_WIDTH)
    kh_ref[...] = (by_chunk(k_t) * g_c).reshape(tt, RWKV_WIDTH)
    v_ref[...] = _swap_halves(v)
    gc_ref[...] = jnp.broadcast_to(g_c, (n_chunks, 8, RWKV_WIDTH)).reshape(n_chunks * 8, RWKV_WIDTH)

    gi = _iota((2 * CHUNK, 2 * CHUNK), 0)
    gj = _iota((2 * CHUNK, 2 * CHUNK), 1) % CHUNK
    g_mask = gj <= jnp.where(gi < CHUNK, gi, gi - (CHUNK + 1))
    lane = _iota((CHUNK, PAIR), 1)
    low = lane < HEAD_DIM
    own = (low, jnp.logical_not(low))
    zeros_cp = jnp.zeros((CHUNK, PAIR), F32)
    cast = (lambda t: t.astype(BF16)) if p_chunk == 1 else (lambda t: t)
    n_pairs = HEADS // 2

    def block(c, size):
        return pl.ds(c * size, size) if isinstance(c, int) else pl.ds(pl.multiple_of(c * size, size), size)

    def phase_a(c0):
        chunks = range(CHUNK_GROUP)
        cps = [(c, p) for c in chunks for p in range(n_pairs)]
        chs = [(c, h) for c in chunks for h in range(HEADS)]
        ld = lambda ref, c, p: ref[block(c0 + c, CHUNK), p * PAIR:(p + 1) * PAIR]
        rt = {(c, p): ld(rt_ref, c, p) for c, p in cps}
        at = {(c, p): ld(at_ref, c, p) for c, p in cps}
        vs = {(c, p): ld(v_ref, c, p) for c, p in cps}

        def key_major(first_ref, second_ref, c, p):
            return cast(jnp.concatenate([ld(first_ref, c, p), ld(second_ref, c, p)], axis=0).T)

        bk = {(c, p): key_major(bt_ref, kt_ref, c, p) for c, p in cps}
        bkt = {(c, p): key_major(bh_ref, kh_ref, c, p) for c, p in cps}

        g, x, vo, m = {}, {}, {}, {}
        for c, h in chs:
            p, s = divmod(h, 2)
            lhs = jnp.concatenate([jnp.where(own[s], rt[c, p], 0.0), jnp.where(own[s], at[c, p], 0.0)], axis=0)
            g[c, h] = cast(jnp.where(g_mask, _mm(cast(lhs), bk[c, p], p_chunk), 0.0))
        for c, h in chs:
            p, s = divmod(h, 2)
            vo[c, h] = cast(jnp.where(own[s], 0.0, vs[c, p]))
            vak = _mm(g[c, h][CHUNK:, :], jnp.concatenate([cast(zeros_cp), vo[c, h]], axis=0), p_chunk)
            x[c, h] = jnp.where(own[s], at[c, p], 0.0) + vak
            m[c, h] = g[c, h][CHUNK:, :CHUNK]
        for step in range(5):
            for ch in chs:
                zz = _mm(m[ch], jnp.concatenate([cast(x[ch]), m[ch]], axis=1), p_chunk)
                x[ch] = x[ch] + zz[:, :PAIR]
                m[ch] = cast(zz[:, PAIR:])
        for ch in chs:
            x[ch] = x[ch] + _mm(m[ch], cast(x[ch]), p_chunk)
        o9 = {}
        for c, h in chs:
            p, s = divmod(h, 2)
            lhs = jnp.concatenate([g[c, h][:CHUNK, :], bkt[c, p][s * CHUNK:(s + 1) * CHUNK, :]], axis=0)
            o9[c, h] = _mm(lhs, jnp.concatenate([cast(x[c, h]), vo[c, h]], axis=0), p_chunk)
        out = {}
        for c, p in cps:
            t0, t1 = o9[c, 2 * p][:CHUNK], o9[c, 2 * p + 1][:CHUNK]
            b0, b1 = o9[c, 2 * p][CHUNK:], o9[c, 2 * p + 1][CHUNK:]
            r_p = rt[c, p] + jnp.where(low, t0, t1)
            y0 = jnp.where(low, t1, t0)
            phi = jnp.concatenate([jnp.where(low, b0, 0.0), jnp.where(low, 0.0, b1)], axis=0)
            hk = jnp.concatenate([jnp.where(low, 0.0, b0), jnp.where(low, b1, 0.0)], axis=0)
            out[c, p] = (cast(jnp.concatenate([r_p, phi], axis=0)), y0, hk)
        return out

    def phase_b(c0, c, prepared, states):
        new_states = []
        for p in range(n_pairs):
            lhs, y0, hk = prepared[c, p]
            gc = gc_ref[block(c0 + c, 8), p * PAIR:(p + 1) * PAIR][0:1, :]
            g_col = jnp.broadcast_to(gc, (PAIR, PAIR)).T
            zz = _mm(lhs, states[p], p_state)
            yc_ref[block(c0 + c, CHUNK), p * PAIR:(p + 1) * PAIR] = zz[:CHUNK] + y0
            new_states.append(g_col * states[p] + (zz[CHUNK:] + hk))
        return new_states

    def group_body(gi, carry):
        c0 = gi * CHUNK_GROUP
        prepared = phase_a(c0)
        states = [h_ref[p] for p in range(n_pairs)]
        for c in range(CHUNK_GROUP):
            states = phase_b(c0, c, prepared, states)
        for p in range(n_pairs):
            h_ref[p] = states[p]
        return carry

    for gi in range(n_chunks // CHUNK_GROUP):
        group_body(gi, 0)

    y = _swap_halves(yc_ref[...])
    mean = _seg_sum(y, ones_bd) * (1.0 / HEAD_DIM)
    yc = y - mean
    var = _seg_sum(yc * yc, ones_bd) * (1.0 / HEAD_DIM)
    yn = yc * lax.rsqrt(var + LNX_EPS) * lng_ref[...] + lnb_ref[...]
    bonus = _seg_sum(r * k_h * rk_ref[...], ones_bd)
    y_ref[...] = yn + bonus * v


def _wkv(zs, mu, w0, w_up, a0, a_up, k_k, k_a, r_k, lnx_g, lnx_b, bsz, seq, tt,
         p_small=1, p_chunk=1, p_state=1):
    n_t = seq // tt
    const = lambda b, i: (0, 0)
    row = lambda w: pl.BlockSpec((1, w), const)
    wide = pltpu.VMEM((tt, RWKV_WIDTH), F32)
    return pl.pallas_call(
        functools.partial(_wkv_kernel, tt=tt, p_small=p_small, p_chunk=p_chunk, p_state=p_state),
        grid=(bsz, n_t),
        in_specs=[pl.BlockSpec((tt, SHIFT_WIDTH), lambda b, i: (b * n_t + i, 0)),
                  row(SHIFT_WIDTH), row(RWKV_WIDTH),
                  pl.BlockSpec((LORA, RWKV_WIDTH), const),
                  row(RWKV_WIDTH),
                  pl.BlockSpec((LORA, RWKV_WIDTH), const),
                  row(RWKV_WIDTH), row(RWKV_WIDTH), row(RWKV_WIDTH), row(RWKV_WIDTH), row(RWKV_WIDTH)],
        out_specs=pl.BlockSpec((tt, RWKV_WIDTH), lambda b, i: (b * n_t + i, 0)),
        out_shape=jax.ShapeDtypeStruct((bsz * seq, RWKV_WIDTH), F32),
        scratch_shapes=[pltpu.VMEM((1, SHIFT_WIDTH), F32),
                        pltpu.VMEM((HEADS // 2, PAIR, PAIR), F32),
                        wide, wide, wide, wide, wide, wide, wide,
                        pltpu.VMEM((tt // CHUNK * 8, RWKV_WIDTH), F32), wide],
        compiler_params=pltpu.CompilerParams(dimension_semantics=("arbitrary", "arbitrary"),
                                             vmem_limit_bytes=VMEM_LIMIT_BYTES),
        name="wkv7",
    )(zs, mu, w0, w_up, a0, a_up, k_k, k_a, r_k, lnx_g, lnx_b)


def _s5_kernel(u_ref, w_ref, cm_ref, lam2_ref, d_ref, y_ref, ut_ref, prev_ref, carry_ref, st_ref, *, rows):
    @pl.when(pl.program_id(0) == 0)
    def _():
        prev_ref[...] = jnp.zeros_like(prev_ref)
        carry_ref[...] = jnp.zeros_like(carry_ref)

    frames = rows // S5_BATCH
    n_col = S5_WIDTH // LANES
    for b in range(S5_BATCH):
        for j in range(n_col):
            col = b * S5_WIDTH + j * LANES
            ut_ref[j, pl.ds(b, frames, stride=S5_BATCH), :] = u_ref[:, col:col + LANES]
    u = jnp.concatenate([ut_ref[j] for j in range(n_col)], axis=1)
    hs = S5_HALF_STATE
    shifted = pltpu.roll(u, S5_BATCH, 0)
    row8 = _iota((SCAN_ROWS, S5_WIDTH), 0)
    head = jnp.where(row8 < S5_BATCH, pltpu.roll(prev_ref[...], S5_BATCH, 0), shifted[:SCAN_ROWS])
    u_prev = jnp.concatenate([head, shifted[SCAN_ROWS:]], axis=0)
    prev_ref[...] = u[rows - SCAN_ROWS:, :]
    ub = u.astype(BF16)
    upb = u_prev.astype(BF16)
    for half in range(S5_PARTS):
        cols = slice(half * S5_HALF, (half + 1) * S5_HALF)
        st_ref[:, half * 2 * hs:(half + 1) * 2 * hs] = jnp.dot(
            jnp.concatenate([ub[:, cols], upb[:, cols]], axis=1), w_ref[half], preferred_element_type=F32)

    n_blocks = rows // SCAN_ROWS
    for half in range(S5_PARTS):
        re_l = pl.ds(half * 2 * hs, hs)
        im_l = pl.ds(half * 2 * hs + hs, hs)
        l_re = lam2_ref[:, re_l]
        l_im = lam2_ref[:, im_l]

        def block(i, carry, re_l=re_l, im_l=im_l, l_re=l_re, l_im=l_im):
            c_re, c_im = carry
            blk = pl.ds(i * SCAN_ROWS, SCAN_ROWS)
            x_re = st_ref[blk, re_l] + (l_re * c_re - l_im * c_im)
            x_im = st_ref[blk, im_l] + (l_re * c_im + l_im * c_re)
            st_ref[blk, re_l] = x_re
            st_ref[blk, im_l] = x_im
            return x_re, x_im

        carry = (carry_ref[:, re_l], carry_ref[:, im_l])
        for i in range(n_blocks):
            carry = block(i, carry)
        c_re, c_im = carry
        carry_ref[:, re_l] = c_re
        carry_ref[:, im_l] = c_im

    outs = [jnp.dot(st_ref[:, half * 2 * hs:(half + 1) * 2 * hs].astype(BF16), cm_ref[half],
                    preferred_element_type=F32) for half in range(S5_PARTS)]
    y = jnp.concatenate(outs, axis=1) + d_ref[...] * u
    for j in range(n_col):
        ut_ref[j] = y[:, j * LANES:(j + 1) * LANES]
    for b in range(S5_BATCH):
        for j in range(n_col):
            col = b * S5_WIDTH + j * LANES
            y_ref[:, col:col + LANES] = ut_ref[j, pl.ds(b, frames, stride=S5_BATCH), :]


def _s5(u_lb, wcat, cmat, lam2, d_row, rows):
    seq = u_lb.shape[0]
    frames = rows // S5_BATCH
    hs = S5_HALF_STATE
    const2 = lambda i: (0, 0)
    const3 = lambda i: (0, 0, 0)
    return pl.pallas_call(
        functools.partial(_s5_kernel, rows=rows),
        grid=(seq // frames,),
        in_specs=[pl.BlockSpec((frames, S5_BATCH * S5_WIDTH), lambda i: (i, 0)),
                  pl.BlockSpec((S5_PARTS, 2 * S5_HALF, 2 * hs), const3),
                  pl.BlockSpec((S5_PARTS, 2 * hs, S5_HALF), const3),
                  pl.BlockSpec((SCAN_ROWS, 2 * S5_PARTS * hs), const2),
                  pl.BlockSpec((1, S5_WIDTH), const2)],
        out_specs=pl.BlockSpec((frames, S5_BATCH * S5_WIDTH), lambda i: (i, 0)),
        out_shape=jax.ShapeDtypeStruct((seq, S5_BATCH * S5_WIDTH), F32),
        scratch_shapes=[pltpu.VMEM((S5_WIDTH // LANES, rows, LANES), F32),
                        pltpu.VMEM((SCAN_ROWS, S5_WIDTH), F32),
                        pltpu.VMEM((SCAN_ROWS, 2 * S5_PARTS * hs), F32),
                        pltpu.VMEM((rows, 2 * S5_PARTS * hs), F32)],
        compiler_params=pltpu.CompilerParams(dimension_semantics=("arbitrary",),
                                             vmem_limit_bytes=VMEM_LIMIT_BYTES),
        name="s5",
    )(u_lb, wcat, cmat, lam2, d_row)


def _s5_params(lam_re, lam_im, log_dt, b_re, b_im, c_re, c_im):
    dt = jnp.exp(log_dt)[:, None]
    mag = jnp.exp(lam_re * dt)
    ang = lam_im * dt
    ab_re = mag * jnp.cos(ang)
    ab_im = mag * jnp.sin(ang)
    den = lam_re * lam_re + lam_im * lam_im
    nr = ab_re - 1.0
    f_re = (nr * lam_re + ab_im * lam_im) / den
    f_im = (ab_im * lam_re - nr * lam_im) / den
    bb_re = f_re[..., None] * b_re - f_im[..., None] * b_im
    bb_im = f_re[..., None] * b_im + f_im[..., None] * b_re

    gh = S5_GROUPS // S5_PARTS
    eye = jnp.eye(gh, dtype=F32)

    def in_half(bb):
        return jnp.einsum('gph,gk->ghkp', bb, eye).reshape(gh * S5_GROUP_DIM, gh * S5_STATE)

    def out_half(cc):
        return jnp.einsum('ghp,gk->gpkh', cc, eye).reshape(gh * S5_STATE, gh * S5_GROUP_DIM)

    lb_re = ab_re[..., None] * bb_re - ab_im[..., None] * bb_im
    lb_im = ab_re[..., None] * bb_im + ab_im[..., None] * bb_re
    halves = [slice(i * gh, (i + 1) * gh) for i in range(S5_PARTS)]
    wcat = jnp.stack([jnp.concatenate(
        [jnp.concatenate([in_half(bb_re[s]), in_half(bb_im[s])], axis=1),
         jnp.concatenate([in_half(lb_re[s]), in_half(lb_im[s])], axis=1)], axis=0)
        for s in halves])
    cmat = jnp.stack([jnp.concatenate([out_half(c_re[s]), -out_half(c_im[s])], axis=0)
                      for s in halves])

    l2_re = (mag * mag * jnp.cos(2.0 * ang)).reshape(S5_PARTS, gh * S5_STATE)
    l2_im = (mag * mag * jnp.sin(2.0 * ang)).reshape(S5_PARTS, gh * S5_STATE)
    lam2 = jnp.concatenate([part for i in range(S5_PARTS) for part in (l2_re[i], l2_im[i])])[None, :]
    lam2 = jnp.broadcast_to(lam2, (SCAN_ROWS, 2 * S5_GROUPS * S5_STATE))
    return wcat.astype(BF16), cmat.astype(BF16), lam2


def _out_kernel(x_ref, ya_ref, ga_ref, ys_ref, gb_ref, m_ref, wglu_ref, bglu_ref, pa_ref, pb_ref,
                wout_ref, fg_ref, o_ref):
    y_a = ya_ref[...] * jax.nn.silu(ga_ref[...].astype(F32))
    proj_a = jnp.dot(y_a.astype(BF16), pa_ref[...], preferred_element_type=F32)
    ys = jax.nn.gelu(ys_ref[...])
    glu = jnp.dot(ys.astype(BF16), wglu_ref[...], preferred_element_type=F32) + bglu_ref[...]
    y_b = glu[:, :S5_WIDTH] * jax.nn.sigmoid(glu[:, S5_WIDTH:]) * jax.nn.silu(gb_ref[...].astype(F32))
    proj_b = jnp.dot(y_b.astype(BF16), pb_ref[...], preferred_element_type=F32)
    m = m_ref[...].astype(F32)
    merged = jax.nn.sigmoid(m[:, :D_MODEL]) * proj_a + jax.nn.sigmoid(m[:, D_MODEL:]) * proj_b
    x_new = x_ref[...] + jnp.dot(merged.astype(BF16), wout_ref[...], preferred_element_type=F32)
    o_ref[...] = _rms_norm(x_new, fg_ref[...])


def _out(x2, ya, ga, ys_lb, gb, m, w_glu, b_glu, p_a, p_b, w_out, final_g, tm):
    t = x2.shape[0]
    const = lambda i: (0, 0)
    tile = lambda w: pl.BlockSpec((tm, w), lambda i: (i, 0))
    full = lambda a: pl.BlockSpec(a.shape, const)
    return pl.pallas_call(
        _out_kernel,
        grid=(t // tm,),
        in_specs=[tile(D_MODEL), tile(RWKV_WIDTH), tile(RWKV_WIDTH),
                  _frame_batch_spec(tm, ys_lb.shape[0] // tm), tile(S5_WIDTH),
                  tile(2 * D_MODEL), full(w_glu), full(b_glu), full(p_a), full(p_b), full(w_out),
                  full(final_g)],
        out_specs=tile(D_MODEL),
        out_shape=jax.ShapeDtypeStruct((t, D_MODEL), F32),
        compiler_params=pltpu.CompilerParams(dimension_semantics=("arbitrary",),
                                             vmem_limit_bytes=VMEM_LIMIT_BYTES),
        name="merge_out",
    )(x2, ya, ga, ys_lb, gb, m, w_glu, b_glu, p_a, p_b, w_out, final_g)


def kernel(x, norm_g, w_in, mu_shift, w0, w_up, a0, a_up, k_k, k_a, r_k, lnx_g, lnx_b, lam_re, lam_im,
           log_dt, b_re, b_im, c_re, c_im, d_skip, w_glu, b_glu, p_a, p_b, w_out, final_g):
    bsz, seq, _ = x.shape
    assert norm_g.shape[0] == 1, "single-layer block: the final RMSNorm is fused into the output kernel"
    tm = min(512, seq)
    tt_wkv = min(256, seq)
    assert bsz == S5_BATCH, "the S5 kernel packs 2 frames x 4 batch rows per sublane block"
    rows_s5 = min(1024, seq * bsz)
    x2 = x.reshape(bsz * seq, D_MODEL)
    row = lambda p: p.reshape(1, -1).astype(F32)
    l = 0
    zs, ga, u_lb, gb, m = _inproj(x2, row(norm_g[l]), w_in[l].astype(BF16), tm, bsz, seq)
    ya = _wkv(zs, row(mu_shift[l]), row(w0[l]), w_up[l], row(a0[l]), a_up[l], row(k_k[l]),
              row(k_a[l]), row(r_k[l]), row(lnx_g[l]), row(lnx_b[l]), bsz, seq, tt_wkv)
    wcat, cmat, lam2 = _s5_params(lam_re[l], lam_im[l], log_dt[l], b_re[l], b_im[l], c_re[l], c_im[l])
    ys_lb = _s5(u_lb, wcat, cmat, lam2, row(d_skip[l]), rows_s5)
    out = _out(x2, ya, ga, ys_lb, gb, m, w_glu[l].astype(BF16), row(b_glu[l]), p_a[l].astype(BF16),
               p_b[l].astype(BF16), w_out[l].astype(BF16), row(final_g), tm)
    return out.reshape(bsz, seq, D_MODEL)
```

```python
import functools
import math

import jax
import jax.numpy as jnp
from jax import lax
from jax.experimental import pallas as pl
from jax.experimental.pallas import tpu as pltpu

F32 = jnp.float32
BF16 = jnp.bfloat16

D_MODEL = 1024
HEADS = 16
HEAD_DIM = 64
RWKV_WIDTH = HEADS * HEAD_DIM
LORA = 64
S5_GROUPS = 32
S5_GROUP_DIM = 16
S5_WIDTH = S5_GROUPS * S5_GROUP_DIM
S5_STATE = 64
SHIFT_WIDTH = 3 * RWKV_WIDTH + 2 * LORA
IN_WIDTH = SHIFT_WIDTH + RWKV_WIDTH + 2 * S5_WIDTH + 2 * D_MODEL
RMS_EPS = 1e-6
LNX_EPS = 64e-5

LANES = 128
SUBLANES = 8
CHUNK = 64
PAIR = 2 * HEAD_DIM
PROJ_SLICE = 512
SEG_TILE = 256
S5_PARTS = 4
S5_HALF = S5_WIDTH // S5_PARTS
S5_HALF_STATE = (S5_GROUPS // S5_PARTS) * S5_STATE
SCAN_ROWS = SUBLANES
S5_BATCH = 4

VMEM_LIMIT_BYTES = 56 * 1024 * 1024


def _split_bf16(x):
    hi = x.astype(BF16)
    lo = (x - hi.astype(F32)).astype(BF16)
    return hi, lo


def _bf16_dot(a, b):
    return jnp.dot(a.astype(BF16), b.astype(BF16), preferred_element_type=F32)


def _iota(shape, dim):
    return lax.broadcasted_iota(jnp.int32, shape, dim)


def _seg_sum(x, ones_bd):
    xb = x.astype(BF16)
    cols = [jnp.dot(xb[:, j * SEG_TILE:(j + 1) * SEG_TILE], ones_bd, preferred_element_type=F32)
            for j in range(x.shape[1] // SEG_TILE)]
    return jnp.concatenate(cols, axis=1)


def _swap_halves(x):
    n = x.shape[1]
    low = (_iota(x.shape, 1) % PAIR) < HEAD_DIM
    return jnp.where(low, pltpu.roll(x, n - HEAD_DIM, 1), pltpu.roll(x, HEAD_DIM, 1))


def _rms_norm(x, g):
    return x * lax.rsqrt(jnp.mean(x * x, axis=-1, keepdims=True) + RMS_EPS) * g


_IN_SPLITS = (0, SHIFT_WIDTH, SHIFT_WIDTH + RWKV_WIDTH, SHIFT_WIDTH + RWKV_WIDTH + S5_WIDTH,
              SHIFT_WIDTH + RWKV_WIDTH + 2 * S5_WIDTH, IN_WIDTH)


def _seg_matrix(swapped):
    seg_i = _iota((SEG_TILE, SEG_TILE), 0) // HEAD_DIM
    seg_j = _iota((SEG_TILE, SEG_TILE), 1) // HEAD_DIM
    if swapped:
        seg_j = seg_j + 1 - 2 * (seg_j % 2)
    return jnp.where(seg_i == seg_j, 1.0, 0.0).astype(BF16)


def _prep_kernel(x_ref, g_ref, w_ref, mu_ref, w0_ref, wup_ref, a0_ref, aup_ref, kk_ref, ka_ref, rk_ref,
                 rt_ref, at_ref, bk_ref, bkt_ref, v_ref, gc_ref, bonus_ref, ga_ref, u_ref, gb_ref, m_ref,
                 prev_ref, *, tm, n_t):
    n_chunks = tm // CHUNK

    @pl.when(pl.program_id(0) % n_t == 0)
    def _():
        prev_ref[...] = jnp.zeros_like(prev_ref)

    h = _rms_norm(x_ref[...], g_ref[...]).astype(BF16)
    proj = lambda lo, hi: jnp.dot(h, w_ref[:, lo:hi], preferred_element_type=F32)

    out_cols = []
    for o_ref, lo, hi in zip((ga_ref, u_ref, gb_ref, m_ref), _IN_SPLITS[1:-1], _IN_SPLITS[2:]):
        for c0 in range(lo, hi, PROJ_SLICE):
            out_cols.append((o_ref, c0 - lo, c0, min(c0 + PROJ_SLICE, hi)))
    out_cols = iter(out_cols)

    def project(n=1):
        for _ in range(n):
            item = next(out_cols, None)
            if item is not None:
                o_ref, at_col, lo, hi = item
                o_ref[:, at_col:at_col + hi - lo] = proj(lo, hi).astype(o_ref.dtype)

    row0 = _iota((tm, 1), 0) == 0

    def shifted(z, lo, hi):
        z_prev = jnp.where(row0, prev_ref[:, lo:hi], pltpu.roll(z, 1, 0))
        return z + mu_ref[:, lo:hi] * (z_prev - z), z[tm - 1:tm, :]

    x_lo = 3 * RWKV_WIDTH
    zx, last_x = shifted(proj(x_lo, SHIFT_WIDTH), x_lo, SHIFT_WIDTH)
    zk_raw = proj(RWKV_WIDTH, 2 * RWKV_WIDTH)
    wl = w0_ref[...] + _bf16_dot(jnp.tanh(zx[:, :LORA]), wup_ref[...])
    logw = -math.exp(-0.5) * jax.nn.sigmoid(wl)
    iclr = jax.nn.sigmoid(a0_ref[...] + _bf16_dot(zx[:, LORA:], aup_ref[...]))
    zr_raw = proj(0, RWKV_WIDTH)
    ti = _iota((tm, tm), 0)
    si = _iota((tm, tm), 1)
    tril_bd = jnp.where(((ti // CHUNK) == (si // CHUNK)) & (si <= ti), 1.0, 0.0).astype(BF16)
    lw_hi, lw_lo = _split_bf16(logw)
    cum = (jnp.dot(tril_bd, lw_hi, preferred_element_type=F32)
           + jnp.dot(tril_bd, lw_lo, preferred_element_type=F32))
    k, last_k = shifted(zk_raw, RWKV_WIDTH, 2 * RWKV_WIDTH)
    zv_raw = proj(2 * RWKV_WIDTH, 3 * RWKV_WIDTH)
    kk = k * kk_ref[...]
    kk = kk * lax.rsqrt(jnp.maximum(_seg_sum(kk * kk, _seg_matrix(False)), 1e-24))
    k_h = k * (1.0 + (iclr - 1.0) * ka_ref[...])
    b = kk * iclr
    project(2)
    by_chunk = lambda t: t.reshape(n_chunks, CHUNK, RWKV_WIDTH)
    g_c = jnp.exp(by_chunk(cum)[:, CHUNK - 1:CHUNK, :])
    gc_ref[...] = jnp.broadcast_to(g_c, (n_chunks, SUBLANES, RWKV_WIDTH)).reshape(n_chunks * SUBLANES, RWKV_WIDTH)
    r, last_r = shifted(zr_raw, 0, RWKV_WIDTH)
    rt_ref[...] = r * jnp.exp(cum)
    project(2)
    at_ref[...] = -kk * jnp.exp(cum - logw)
    bonus_ref[...] = _seg_sum(r * k_h * rk_ref[...], _seg_matrix(True))
    project(2)
    v, last_v = shifted(zv_raw, 2 * RWKV_WIDTH, 3 * RWKV_WIDTH)
    v_ref[...] = _swap_halves(v)
    prev_ref[...] = jnp.concatenate([last_r, last_k, last_v, last_x], axis=1)
    project(2)
    g_inv = jnp.exp(-cum)
    b_t = b * g_inv
    k_t = k_h * g_inv
    b_hat = (by_chunk(b_t) * g_c).reshape(tm, RWKV_WIDTH)
    k_hat = (by_chunk(k_t) * g_c).reshape(tm, RWKV_WIDTH)
    project(2)
    for c in range(n_chunks):
        for p in range(HEADS // 2):
            blk = (slice(c * CHUNK, (c + 1) * CHUNK), slice(p * PAIR, (p + 1) * PAIR))
            dst = slice((c * (HEADS // 2) + p) * PAIR, (c * (HEADS // 2) + p + 1) * PAIR)
            bk_ref[dst, :] = jnp.concatenate([b_t[blk], k_t[blk]], axis=0).T.astype(BF16)
            bkt_ref[dst, :] = jnp.concatenate([b_hat[blk], k_hat[blk]], axis=0).T.astype(BF16)
        project(1)
    project(64)


def _frame_batch_spec(tm, n_t):
    return pl.BlockSpec((tm, S5_WIDTH), lambda i: (i % n_t, i // n_t))


def _prep(x2, norm_g, w_in_bf16, mu, w0, w_up, a0, a_up, k_k, k_a, r_k, tm, bsz, seq):
    t = x2.shape[0]
    n_t = seq // tm
    const = lambda i: (0, 0)
    rowspec = lambda w: pl.BlockSpec((1, w), const)
    tile = lambda rows, w: pl.BlockSpec((rows, w), lambda i: (i, 0))
    pair_rows = tm // CHUNK * (HEADS // 2) * PAIR
    wide = jax.ShapeDtypeStruct((t, RWKV_WIDTH), F32)
    key_major = jax.ShapeDtypeStruct((t // CHUNK * (HEADS // 2) * PAIR, PAIR), BF16)
    out_shape = [wide, wide, key_major, key_major, wide,
                 jax.ShapeDtypeStruct((t // CHUNK * SUBLANES, RWKV_WIDTH), F32), wide,
                 jax.ShapeDtypeStruct((t, RWKV_WIDTH), BF16),
                 jax.ShapeDtypeStruct((seq, bsz * S5_WIDTH), F32),
                 jax.ShapeDtypeStruct((t, S5_WIDTH), BF16),
                 jax.ShapeDtypeStruct((t, 2 * D_MODEL), BF16)]
    out_specs = [tile(tm, RWKV_WIDTH), tile(tm, RWKV_WIDTH), tile(pair_rows, PAIR), tile(pair_rows, PAIR),
                 tile(tm, RWKV_WIDTH), tile(tm // CHUNK * SUBLANES, RWKV_WIDTH), tile(tm, RWKV_WIDTH),
                 tile(tm, RWKV_WIDTH), _frame_batch_spec(tm, n_t), tile(tm, S5_WIDTH), tile(tm, 2 * D_MODEL)]
    return pl.pallas_call(
        functools.partial(_prep_kernel, tm=tm, n_t=n_t),
        grid=(t // tm,),
        in_specs=[pl.BlockSpec((tm, D_MODEL), lambda i: (i, 0)),
                  rowspec(D_MODEL),
                  pl.BlockSpec((D_MODEL, IN_WIDTH), const, pipeline_mode=pl.Buffered(1)),
                  rowspec(SHIFT_WIDTH), rowspec(RWKV_WIDTH),
                  pl.BlockSpec((LORA, RWKV_WIDTH), const),
                  rowspec(RWKV_WIDTH),
                  pl.BlockSpec((LORA, RWKV_WIDTH), const),
                  rowspec(RWKV_WIDTH), rowspec(RWKV_WIDTH), rowspec(RWKV_WIDTH)],
        out_specs=out_specs,
        out_shape=out_shape,
        scratch_shapes=[pltpu.VMEM((1, SHIFT_WIDTH), F32)],
        compiler_params=pltpu.CompilerParams(dimension_semantics=("arbitrary",),
                                             vmem_limit_bytes=VMEM_LIMIT_BYTES),
        name="inproj_prep",
    )(x2, norm_g, w_in_bf16, mu, w0, w_up, a0, a_up, k_k, k_a, r_k)


def _wkv_kernel(rt_ref, at_ref, bk_ref, bkt_ref, v_ref, gc_ref, bonus_ref, lng_ref, lnb_ref, y_ref,
                h_ref, yc_ref, *, tt):
    n_chunks = tt // CHUNK

    @pl.when(pl.program_id(1) == 0)
    def _():
        h_ref[...] = jnp.zeros_like(h_ref)

    gi = _iota((2 * CHUNK, 2 * CHUNK), 0)
    gj = _iota((2 * CHUNK, 2 * CHUNK), 1) % CHUNK
    g_mask = gj <= jnp.where(gi < CHUNK, gi, gi - (CHUNK + 1))
    lane = _iota((CHUNK, PAIR), 1)
    low = lane < HEAD_DIM
    own = (low, jnp.logical_not(low))
    zeros_cp = jnp.zeros((CHUNK, PAIR), F32)
    cast = lambda t: t.astype(BF16)
    n_pairs = HEADS // 2

    def block(c, size):
        return pl.ds(c * size, size) if isinstance(c, int) else pl.ds(pl.multiple_of(c * size, size), size)

    def phase_a(c0):
        chunks = (0,)
        cps = [(c, p) for c in chunks for p in range(n_pairs)]
        chs = [(c, h) for c in chunks for h in range(HEADS)]
        ld = lambda ref, c, p: ref[block(c0 + c, CHUNK), p * PAIR:(p + 1) * PAIR]
        rt = {(c, p): ld(rt_ref, c, p) for c, p in cps}
        at = {(c, p): ld(at_ref, c, p) for c, p in cps}
        vs = {(c, p): ld(v_ref, c, p) for c, p in cps}

        key_major = lambda ref, c, p: ref[block((c0 + c) * n_pairs + p, PAIR), :]
        bk = {(c, p): key_major(bk_ref, c, p) for c, p in cps}
        bkt = {(c, p): key_major(bkt_ref, c, p) for c, p in cps}

        g, x, vo, m = {}, {}, {}, {}
        for c, h in chs:
            p, s = divmod(h, 2)
            lhs = jnp.concatenate([jnp.where(own[s], rt[c, p], 0.0), jnp.where(own[s], at[c, p], 0.0)], axis=0)
            g[c, h] = cast(jnp.where(g_mask, _bf16_dot(cast(lhs), bk[c, p]), 0.0))
        for c, h in chs:
            p, s = divmod(h, 2)
            vo[c, h] = cast(jnp.where(own[s], 0.0, vs[c, p]))
            vak = _bf16_dot(g[c, h][CHUNK:, :], jnp.concatenate([cast(zeros_cp), vo[c, h]], axis=0))
            x[c, h] = jnp.where(own[s], at[c, p], 0.0) + vak
            m[c, h] = g[c, h][CHUNK:, :CHUNK]
        for step in range(5):
            for ch in chs:
                zz = _bf16_dot(m[ch], jnp.concatenate([cast(x[ch]), m[ch]], axis=1))
                x[ch] = x[ch] + zz[:, :PAIR]
                m[ch] = cast(zz[:, PAIR:])
        for ch in chs:
            x[ch] = x[ch] + _bf16_dot(m[ch], cast(x[ch]))
        o9 = {}
        for c, h in chs:
            p, s = divmod(h, 2)
            lhs = jnp.concatenate([g[c, h][:CHUNK, :], bkt[c, p][s * CHUNK:(s + 1) * CHUNK, :]], axis=0)
            o9[c, h] = _bf16_dot(lhs, jnp.concatenate([cast(x[c, h]), vo[c, h]], axis=0))
        out = {}
        for c, p in cps:
            t0, t1 = o9[c, 2 * p][:CHUNK], o9[c, 2 * p + 1][:CHUNK]
            b0, b1 = o9[c, 2 * p][CHUNK:], o9[c, 2 * p + 1][CHUNK:]
            r_p = rt[c, p] + jnp.where(low, t0, t1)
            y0 = jnp.where(low, t1, t0)
            phi = jnp.concatenate([jnp.where(low, b0, 0.0), jnp.where(low, 0.0, b1)], axis=0)
            hk = jnp.concatenate([jnp.where(low, 0.0, b0), jnp.where(low, b1, 0.0)], axis=0)
            out[c, p] = (cast(jnp.concatenate([r_p, phi], axis=0)), y0, hk)
        return out

    def phase_b(c0, c, prepared, states):
        new_states = []
        for p in range(n_pairs):
            lhs, y0, hk = prepared[c, p]
            gc = gc_ref[block(c0 + c, SUBLANES), p * PAIR:(p + 1) * PAIR][0:1, :]
            g_col = jnp.broadcast_to(gc, (PAIR, PAIR)).T
            zz = _bf16_dot(lhs, states[p])
            yc_ref[block(c0 + c, CHUNK), p * PAIR:(p + 1) * PAIR] = zz[:CHUNK] + y0
            new_states.append(g_col * states[p] + (zz[CHUNK:] + hk))
        return new_states

    states = [h_ref[p] for p in range(n_pairs)]
    for c0 in range(n_chunks):
        states = phase_b(c0, 0, phase_a(c0), states)
    for p in range(n_pairs):
        h_ref[p] = states[p]

    ones_bd = _seg_matrix(False)
    y = yc_ref[...]
    mean = _seg_sum(y, ones_bd) * (1.0 / HEAD_DIM)
    yc = y - mean
    var = _seg_sum(yc * yc, ones_bd) * (1.0 / HEAD_DIM)
    yn = yc * lax.rsqrt(var + LNX_EPS) * lng_ref[...] + lnb_ref[...]
    y_ref[...] = _swap_halves(yn + bonus_ref[...] * v_ref[...])


def _wkv(rt, at, bk, bkt, v_sw, gc, bonus_sw, lnx_g_sw, lnx_b_sw, bsz, seq, tt):
    n_t = seq // tt
    const = lambda b, i: (0, 0)
    tile = lambda rows, w: pl.BlockSpec((rows, w), lambda b, i: (b * n_t + i, 0))
    pair_rows = tt // CHUNK * (HEADS // 2) * PAIR
    return pl.pallas_call(
        functools.partial(_wkv_kernel, tt=tt),
        grid=(bsz, n_t),
        in_specs=[tile(tt, RWKV_WIDTH), tile(tt, RWKV_WIDTH), tile(pair_rows, PAIR), tile(pair_rows, PAIR),
                  tile(tt, RWKV_WIDTH), tile(tt // CHUNK * SUBLANES, RWKV_WIDTH), tile(tt, RWKV_WIDTH),
                  pl.BlockSpec((1, RWKV_WIDTH), const), pl.BlockSpec((1, RWKV_WIDTH), const)],
        out_specs=tile(tt, RWKV_WIDTH),
        out_shape=jax.ShapeDtypeStruct((bsz * seq, RWKV_WIDTH), F32),
        scratch_shapes=[pltpu.VMEM((HEADS // 2, PAIR, PAIR), F32),
                        pltpu.VMEM((tt, RWKV_WIDTH), F32)],
        compiler_params=pltpu.CompilerParams(dimension_semantics=("arbitrary", "arbitrary"),
                                             vmem_limit_bytes=VMEM_LIMIT_BYTES),
        name="wkv7",
    )(rt, at, bk, bkt, v_sw, gc, bonus_sw, lnx_g_sw, lnx_b_sw)


def _s5_kernel(u_ref, w_ref, cm_ref, lam2_ref, d_ref, y_ref, ut_ref, prev_ref, carry_ref, st_ref, *, rows):
    @pl.when(pl.program_id(0) == 0)
    def _():
        prev_ref[...] = jnp.zeros_like(prev_ref)
        carry_ref[...] = jnp.zeros_like(carry_ref)

    frames = rows // S5_BATCH
    n_col = S5_WIDTH // LANES
    for b in range(S5_BATCH):
        for j in range(n_col):
            col = b * S5_WIDTH + j * LANES
            ut_ref[j, pl.ds(b, frames, stride=S5_BATCH), :] = u_ref[:, col:col + LANES]
    u = jnp.concatenate([ut_ref[j] for j in range(n_col)], axis=1)
    hs = S5_HALF_STATE
    shifted = pltpu.roll(u, S5_BATCH, 0)
    row8 = _iota((SCAN_ROWS, S5_WIDTH), 0)
    head = jnp.where(row8 < S5_BATCH, pltpu.roll(prev_ref[...], S5_BATCH, 0), shifted[:SCAN_ROWS])
    u_prev = jnp.concatenate([head, shifted[SCAN_ROWS:]], axis=0)
    prev_ref[...] = u[rows - SCAN_ROWS:, :]
    ub = u.astype(BF16)
    upb = u_prev.astype(BF16)
    for half in range(S5_PARTS):
        cols = slice(half * S5_HALF, (half + 1) * S5_HALF)
        st_ref[:, half * 2 * hs:(half + 1) * 2 * hs] = jnp.dot(
            jnp.concatenate([ub[:, cols], upb[:, cols]], axis=1), w_ref[half], preferred_element_type=F32)

    n_blocks = rows // SCAN_ROWS
    for half in range(S5_PARTS):
        re_l = pl.ds(half * 2 * hs, hs)
        im_l = pl.ds(half * 2 * hs + hs, hs)
        l_re = lam2_ref[:, re_l]
        l_im = lam2_ref[:, im_l]

        def block(i, carry, re_l=re_l, im_l=im_l, l_re=l_re, l_im=l_im):
            c_re, c_im = carry
            blk = pl.ds(i * SCAN_ROWS, SCAN_ROWS)
            x_re = st_ref[blk, re_l] + (l_re * c_re - l_im * c_im)
            x_im = st_ref[blk, im_l] + (l_re * c_im + l_im * c_re)
            st_ref[blk, re_l] = x_re
            st_ref[blk, im_l] = x_im
            return x_re, x_im

        carry = (carry_ref[:, re_l], carry_ref[:, im_l])
        for i in range(n_blocks):
            carry = block(i, carry)
        c_re, c_im = carry
        carry_ref[:, re_l] = c_re
        carry_ref[:, im_l] = c_im

    outs = [jnp.dot(st_ref[:, half * 2 * hs:(half + 1) * 2 * hs].astype(BF16), cm_ref[half],
                    preferred_element_type=F32) for half in range(S5_PARTS)]
    y = jnp.concatenate(outs, axis=1) + d_ref[...] * u
    for j in range(n_col):
        ut_ref[j] = y[:, j * LANES:(j + 1) * LANES]
    for b in range(S5_BATCH):
        for j in range(n_col):
            col = b * S5_WIDTH + j * LANES
            y_ref[:, col:col + LANES] = ut_ref[j, pl.ds(b, frames, stride=S5_BATCH), :]


def _s5(u_lb, wcat, cmat, lam2, d_row, rows):
    seq = u_lb.shape[0]
    frames = rows // S5_BATCH
    hs = S5_HALF_STATE
    const2 = lambda i: (0, 0)
    const3 = lambda i: (0, 0, 0)
    return pl.pallas_call(
        functools.partial(_s5_kernel, rows=rows),
        grid=(seq // frames,),
        in_specs=[pl.BlockSpec((frames, S5_BATCH * S5_WIDTH), lambda i: (i, 0)),
                  pl.BlockSpec((S5_PARTS, 2 * S5_HALF, 2 * hs), const3),
                  pl.BlockSpec((S5_PARTS, 2 * hs, S5_HALF), const3),
                  pl.BlockSpec((SCAN_ROWS, 2 * S5_PARTS * hs), const2),
                  pl.BlockSpec((1, S5_WIDTH), const2)],
        out_specs=pl.BlockSpec((frames, S5_BATCH * S5_WIDTH), lambda i: (i, 0)),
        out_shape=jax.ShapeDtypeStruct((seq, S5_BATCH * S5_WIDTH), F32),
        scratch_shapes=[pltpu.VMEM((S5_WIDTH // LANES, rows, LANES), F32),
                        pltpu.VMEM((SCAN_ROWS, S5_WIDTH), F32),
                        pltpu.VMEM((SCAN_ROWS, 2 * S5_PARTS * hs), F32),
                        pltpu.VMEM((rows, 2 * S5_PARTS * hs), F32)],
        compiler_params=pltpu.CompilerParams(dimension_semantics=("arbitrary",),
                                             vmem_limit_bytes=VMEM_LIMIT_BYTES),
        name="s5",
    )(u_lb, wcat, cmat, lam2, d_row)


def _s5_params(lam_re, lam_im, log_dt, b_re, b_im, c_re, c_im):
    dt = jnp.exp(log_dt)[:, None]
    mag = jnp.exp(lam_re * dt)
    ang = lam_im * dt
    ab_re = mag * jnp.cos(ang)
    ab_im = mag * jnp.sin(ang)
    den = lam_re * lam_re + lam_im * lam_im
    nr = ab_re - 1.0
    f_re = (nr * lam_re + ab_im * lam_im) / den
    f_im = (ab_im * lam_re - nr * lam_im) / den
    bb_re = f_re[..., None] * b_re - f_im[..., None] * b_im
    bb_im = f_re[..., None] * b_im + f_im[..., None] * b_re

    gh = S5_GROUPS // S5_PARTS
    eye = jnp.eye(gh, dtype=F32)

    def in_half(bb):
        return jnp.einsum('gph,gk->ghkp', bb, eye).reshape(gh * S5_GROUP_DIM, gh * S5_STATE)

    def out_half(cc):
        return jnp.einsum('ghp,gk->gpkh', cc, eye).reshape(gh * S5_STATE, gh * S5_GROUP_DIM)

    lb_re = ab_re[..., None] * bb_re - ab_im[..., None] * bb_im
    lb_im = ab_re[..., None] * bb_im + ab_im[..., None] * bb_re
    halves = [slice(i * gh, (i + 1) * gh) for i in range(S5_PARTS)]
    wcat = jnp.stack([jnp.concatenate(
        [jnp.concatenate([in_half(bb_re[s]), in_half(bb_im[s])], axis=1),
         jnp.concatenate([in_half(lb_re[s]), in_half(lb_im[s])], axis=1)], axis=0)
        for s in halves])
    cmat = jnp.stack([jnp.concatenate([out_half(c_re[s]), -out_half(c_im[s])], axis=0)
                      for s in halves])

    l2_re = (mag * mag * jnp.cos(2.0 * ang)).reshape(S5_PARTS, gh * S5_STATE)
    l2_im = (mag * mag * jnp.sin(2.0 * ang)).reshape(S5_PARTS, gh * S5_STATE)
    lam2 = jnp.concatenate([part for i in range(S5_PARTS) for part in (l2_re[i], l2_im[i])])[None, :]
    lam2 = jnp.broadcast_to(lam2, (SCAN_ROWS, 2 * S5_GROUPS * S5_STATE))
    return wcat.astype(BF16), cmat.astype(BF16), lam2


def _out_kernel(x_ref, ya_ref, ga_ref, ys_ref, gb_ref, m_ref, wglu_ref, bglu_ref, pa_ref, pb_ref,
                wout_ref, fg_ref, o_ref):
    y_a = ya_ref[...] * jax.nn.silu(ga_ref[...].astype(F32))
    proj_a = jnp.dot(y_a.astype(BF16), pa_ref[...], preferred_element_type=F32)
    ys = jax.nn.gelu(ys_ref[...])
    glu = jnp.dot(ys.astype(BF16), wglu_ref[...], preferred_element_type=F32) + bglu_ref[...]
    y_b = glu[:, :S5_WIDTH] * jax.nn.sigmoid(glu[:, S5_WIDTH:]) * jax.nn.silu(gb_ref[...].astype(F32))
    proj_b = jnp.dot(y_b.astype(BF16), pb_ref[...], preferred_element_type=F32)
    m = m_ref[...].astype(F32)
    merged = jax.nn.sigmoid(m[:, :D_MODEL]) * proj_a + jax.nn.sigmoid(m[:, D_MODEL:]) * proj_b
    x_new = x_ref[...] + jnp.dot(merged.astype(BF16), wout_ref[...], preferred_element_type=F32)
    o_ref[...] = _rms_norm(x_new, fg_ref[...])


def _out(x2, ya, ga, ys_lb, gb, m, w_glu, b_glu, p_a, p_b, w_out, final_g, tm):
    t = x2.shape[0]
    const = lambda i: (0, 0)
    tile = lambda w: pl.BlockSpec((tm, w), lambda i: (i, 0))
    full = lambda a: pl.BlockSpec(a.shape, const)
    return pl.pallas_call(
        _out_kernel,
        grid=(t // tm,),
        in_specs=[tile(D_MODEL), tile(RWKV_WIDTH), tile(RWKV_WIDTH),
                  _frame_batch_spec(tm, ys_lb.shape[0] // tm), tile(S5_WIDTH),
                  tile(2 * D_MODEL), full(w_glu), full(b_glu), full(p_a), full(p_b), full(w_out),
                  full(final_g)],
        out_specs=tile(D_MODEL),
        out_shape=jax.ShapeDtypeStruct((t, D_MODEL), F32),
        compiler_params=pltpu.CompilerParams(dimension_semantics=("arbitrary",),
                                             vmem_limit_bytes=VMEM_LIMIT_BYTES),
        name="merge_out",
    )(x2, ya, ga, ys_lb, gb, m, w_glu, b_glu, p_a, p_b, w_out, final_g)


def kernel(x, norm_g, w_in, mu_shift, w0, w_up, a0, a_up, k_k, k_a, r_k, lnx_g, lnx_b, lam_re, lam_im,
           log_dt, b_re, b_im, c_re, c_im, d_skip, w_glu, b_glu, p_a, p_b, w_out, final_g):
    bsz, seq, _ = x.shape
    assert norm_g.shape[0] == 1, "single-layer block: the final RMSNorm is fused into the output kernel"
    tm = min(512, seq)
    tt_wkv = min(256, seq)
    assert bsz == S5_BATCH, "the S5 kernel packs 2 frames x 4 batch rows per sublane block"
    rows_s5 = min(1024, seq * bsz)
    x2 = x.reshape(bsz * seq, D_MODEL)
    row = lambda p: p.reshape(1, -1).astype(F32)
    l = 0
    (rt, at, bk, bkt, v_sw, gc, bonus_sw, ga, u_lb, gb, m) = _prep(
        x2, row(norm_g[l]), w_in[l].astype(BF16), row(mu_shift[l]), row(w0[l]), w_up[l], row(a0[l]), a_up[l],
        row(k_k[l]), row(k_a[l]), row(r_k[l]), tt_wkv, bsz, seq)
    swap_row = lambda p: row(p).reshape(HEADS // 2, 2, HEAD_DIM)[:, ::-1].reshape(1, RWKV_WIDTH)
    ya = _wkv(rt, at, bk, bkt, v_sw, gc, bonus_sw, swap_row(lnx_g[l]), swap_row(lnx_b[l]), bsz, seq, tt_wkv)
    wcat, cmat, lam2 = _s5_params(lam_re[l], lam_im[l], log_dt[l], b_re[l], b_im[l], c_re[l], c_im[l])
    ys_lb = _s5(u_lb, wcat, cmat, lam2, row(d_skip[l]), rows_s5)
    out = _out(x2, ya, ga, ys_lb, gb, m, w_glu[l].astype(BF16), row(b_glu[l]), p_a[l].astype(BF16),
               p_b[l].astype(BF16), w_out[l].astype(BF16), row(final_g), tm)
    return out.reshape(bsz, seq, D_MODEL)
```

```python
import functools
import math

import jax
import jax.numpy as jnp
from jax import lax
from jax.experimental import pallas as pl
from jax.experimental.pallas import tpu as pltpu

F32 = jnp.float32
BF16 = jnp.bfloat16

D_MODEL = 1024
HEADS = 16
HEAD_DIM = 64
RWKV_WIDTH = HEADS * HEAD_DIM
LORA = 64
S5_GROUPS = 32
S5_GROUP_DIM = 16
S5_WIDTH = S5_GROUPS * S5_GROUP_DIM
S5_STATE = 64
SHIFT_WIDTH = 3 * RWKV_WIDTH + 2 * LORA
IN_WIDTH = SHIFT_WIDTH + RWKV_WIDTH + 2 * S5_WIDTH + 2 * D_MODEL
RMS_EPS = 1e-6
LNX_EPS = 64e-5

LANES = 128
SUBLANES = 8
CHUNK = 64
PAIR = 2 * HEAD_DIM
CHUNK_GROUP = 2
PROJ_SLICE = 512
SEG_TILE = 256
S5_PARTS = 4
S5_HALF = S5_WIDTH // S5_PARTS
S5_HALF_STATE = (S5_GROUPS // S5_PARTS) * S5_STATE
SCAN_ROWS = SUBLANES
S5_BATCH = 4

VMEM_LIMIT_BYTES = 56 * 1024 * 1024


def _split_bf16(x):
    hi = x.astype(BF16)
    lo = (x - hi.astype(F32)).astype(BF16)
    return hi, lo


def _bf16_dot(a, b):
    return jnp.dot(a.astype(BF16), b.astype(BF16), preferred_element_type=F32)


def _iota(shape, dim):
    return lax.broadcasted_iota(jnp.int32, shape, dim)


def _seg_sum(x, ones_bd):
    xb = x.astype(BF16)
    cols = [jnp.dot(xb[:, j * SEG_TILE:(j + 1) * SEG_TILE], ones_bd, preferred_element_type=F32)
            for j in range(x.shape[1] // SEG_TILE)]
    return jnp.concatenate(cols, axis=1)


def _swap_halves(x):
    n = x.shape[1]
    low = (_iota(x.shape, 1) % PAIR) < HEAD_DIM
    return jnp.where(low, pltpu.roll(x, n - HEAD_DIM, 1), pltpu.roll(x, HEAD_DIM, 1))


def _rms_norm(x, g):
    return x * lax.rsqrt(jnp.mean(x * x, axis=-1, keepdims=True) + RMS_EPS) * g


_IN_SPLITS = (0, SHIFT_WIDTH, SHIFT_WIDTH + RWKV_WIDTH, SHIFT_WIDTH + RWKV_WIDTH + S5_WIDTH,
              SHIFT_WIDTH + RWKV_WIDTH + 2 * S5_WIDTH, IN_WIDTH)


def _seg_matrix(swapped):
    seg_i = _iota((SEG_TILE, SEG_TILE), 0) // HEAD_DIM
    seg_j = _iota((SEG_TILE, SEG_TILE), 1) // HEAD_DIM
    if swapped:
        seg_j = seg_j + 1 - 2 * (seg_j % 2)
    return jnp.where(seg_i == seg_j, 1.0, 0.0).astype(BF16)


def _prep_kernel(x_ref, g_ref, w_ref, mu_ref, w0_ref, wup_ref, a0_ref, aup_ref, kk_ref, ka_ref, rk_ref,
                 rt_ref, at_ref, bk_ref, bkt_ref, v_ref, gc_ref, bonus_ref, ga_ref, u_ref, gb_ref, m_ref,
                 prev_ref, *, tm, n_t):
    n_chunks = tm // CHUNK

    @pl.when(pl.program_id(0) % n_t == 0)
    def _():
        prev_ref[...] = jnp.zeros_like(prev_ref)

    h = _rms_norm(x_ref[...], g_ref[...]).astype(BF16)
    proj = lambda lo, hi: jnp.dot(h, w_ref[:, lo:hi], preferred_element_type=F32)

    out_cols = []
    for o_ref, lo, hi in zip((ga_ref, u_ref, gb_ref, m_ref), _IN_SPLITS[1:-1], _IN_SPLITS[2:]):
        for c0 in range(lo, hi, PROJ_SLICE):
            out_cols.append((o_ref, c0 - lo, c0, min(c0 + PROJ_SLICE, hi)))
    out_cols = iter(out_cols)

    def project(n=1):
        for _ in range(n):
            item = next(out_cols, None)
            if item is not None:
                o_ref, at_col, lo, hi = item
                o_ref[:, at_col:at_col + hi - lo] = proj(lo, hi).astype(o_ref.dtype)

    row0 = _iota((tm, 1), 0) == 0

    def shifted(z, lo, hi):
        z_prev = jnp.where(row0, prev_ref[:, lo:hi], pltpu.roll(z, 1, 0))
        return z + mu_ref[:, lo:hi] * (z_prev - z), z[tm - 1:tm, :]

    x_lo = 3 * RWKV_WIDTH
    zx, last_x = shifted(proj(x_lo, SHIFT_WIDTH), x_lo, SHIFT_WIDTH)
    zk_raw = proj(RWKV_WIDTH, 2 * RWKV_WIDTH)
    wl = w0_ref[...] + _bf16_dot(jnp.tanh(zx[:, :LORA]), wup_ref[...])
    logw = -math.exp(-0.5) * jax.nn.sigmoid(wl)
    iclr = jax.nn.sigmoid(a0_ref[...] + _bf16_dot(zx[:, LORA:], aup_ref[...]))
    zr_raw = proj(0, RWKV_WIDTH)
    ti = _iota((tm, tm), 0)
    si = _iota((tm, tm), 1)
    tril_bd = jnp.where(((ti // CHUNK) == (si // CHUNK)) & (si <= ti), 1.0, 0.0).astype(BF16)
    lw_hi, lw_lo = _split_bf16(logw)
    cum = (jnp.dot(tril_bd, lw_hi, preferred_element_type=F32)
           + jnp.dot(tril_bd, lw_lo, preferred_element_type=F32))
    k, last_k = shifted(zk_raw, RWKV_WIDTH, 2 * RWKV_WIDTH)
    zv_raw = proj(2 * RWKV_WIDTH, 3 * RWKV_WIDTH)
    kk = k * kk_ref[...]
    kk = kk * lax.rsqrt(jnp.maximum(_seg_sum(kk * kk, _seg_matrix(False)), 1e-24))
    k_h = k * (1.0 + (iclr - 1.0) * ka_ref[...])
    b = kk * iclr
    project(2)
    by_chunk = lambda t: t.reshape(n_chunks, CHUNK, RWKV_WIDTH)
    g_c = jnp.exp(by_chunk(cum)[:, CHUNK - 1:CHUNK, :])
    gc_ref[...] = jnp.broadcast_to(g_c, (n_chunks, SUBLANES, RWKV_WIDTH)).reshape(n_chunks * SUBLANES, RWKV_WIDTH)
    r, last_r = shifted(zr_raw, 0, RWKV_WIDTH)
    rt_ref[...] = r * jnp.exp(cum)
    project(2)
    at_ref[...] = -kk * jnp.exp(cum - logw)
    bonus_ref[...] = _seg_sum(r * k_h * rk_ref[...], _seg_matrix(True))
    project(2)
    v, last_v = shifted(zv_raw, 2 * RWKV_WIDTH, 3 * RWKV_WIDTH)
    v_ref[...] = _swap_halves(v)
    prev_ref[...] = jnp.concatenate([last_r, last_k, last_v, last_x], axis=1)
    project(2)
    g_inv = jnp.exp(-cum)
    b_t = b * g_inv
    k_t = k_h * g_inv
    b_hat = (by_chunk(b_t) * g_c).reshape(tm, RWKV_WIDTH)
    k_hat = (by_chunk(k_t) * g_c).reshape(tm, RWKV_WIDTH)
    project(2)
    for c in range(n_chunks):
        for p in range(HEADS // 2):
            blk = (slice(c * CHUNK, (c + 1) * CHUNK), slice(p * PAIR, (p + 1) * PAIR))
            dst = slice((c * (HEADS // 2) + p) * PAIR, (c * (HEADS // 2) + p + 1) * PAIR)
            bk_ref[dst, :] = jnp.concatenate([b_t[blk], k_t[blk]], axis=0).T.astype(BF16)
            bkt_ref[dst, :] = jnp.concatenate([b_hat[blk], k_hat[blk]], axis=0).T.astype(BF16)
        project(1)
    project(64)


def _frame_batch_spec(tm, n_t):
    return pl.BlockSpec((tm, S5_WIDTH), lambda i: (i % n_t, i // n_t))


def _prep(x2, norm_g, w_in_bf16, mu, w0, w_up, a0, a_up, k_k, k_a, r_k, tm, bsz, seq):
    t = x2.shape[0]
    n_t = seq // tm
    const = lambda i: (0, 0)
    rowspec = lambda w: pl.BlockSpec((1, w), const)
    tile = lambda rows, w: pl.BlockSpec((rows, w), lambda i: (i, 0))
    pair_rows = tm // CHUNK * (HEADS // 2) * PAIR
    wide = jax.ShapeDtypeStruct((t, RWKV_WIDTH), F32)
    key_major = jax.ShapeDtypeStruct((t // CHUNK * (HEADS // 2) * PAIR, PAIR), BF16)
    out_shape = [wide, wide, key_major, key_major, wide,
                 jax.ShapeDtypeStruct((t // CHUNK * SUBLANES, RWKV_WIDTH), F32), wide,
                 jax.ShapeDtypeStruct((t, RWKV_WIDTH), BF16),
                 jax.ShapeDtypeStruct((seq, bsz * S5_WIDTH), F32),
                 jax.ShapeDtypeStruct((t, S5_WIDTH), BF16),
                 jax.ShapeDtypeStruct((t, 2 * D_MODEL), BF16)]
    out_specs = [tile(tm, RWKV_WIDTH), tile(tm, RWKV_WIDTH), tile(pair_rows, PAIR), tile(pair_rows, PAIR),
                 tile(tm, RWKV_WIDTH), tile(tm // CHUNK * SUBLANES, RWKV_WIDTH), tile(tm, RWKV_WIDTH),
                 tile(tm, RWKV_WIDTH), _frame_batch_spec(tm, n_t), tile(tm, S5_WIDTH), tile(tm, 2 * D_MODEL)]
    return pl.pallas_call(
        functools.partial(_prep_kernel, tm=tm, n_t=n_t),
        grid=(t // tm,),
        in_specs=[pl.BlockSpec((tm, D_MODEL), lambda i: (i, 0)),
                  rowspec(D_MODEL),
                  pl.BlockSpec((D_MODEL, IN_WIDTH), const, pipeline_mode=pl.Buffered(1)),
                  rowspec(SHIFT_WIDTH), rowspec(RWKV_WIDTH),
                  pl.BlockSpec((LORA, RWKV_WIDTH), const),
                  rowspec(RWKV_WIDTH),
                  pl.BlockSpec((LORA, RWKV_WIDTH), const),
                  rowspec(RWKV_WIDTH), rowspec(RWKV_WIDTH), rowspec(RWKV_WIDTH)],
        out_specs=out_specs,
        out_shape=out_shape,
        scratch_shapes=[pltpu.VMEM((1, SHIFT_WIDTH), F32)],
        compiler_params=pltpu.CompilerParams(dimension_semantics=("arbitrary",),
                                             vmem_limit_bytes=VMEM_LIMIT_BYTES),
        name="inproj_prep",
    )(x2, norm_g, w_in_bf16, mu, w0, w_up, a0, a_up, k_k, k_a, r_k)


def _wkv_kernel(rt_ref, at_ref, bk_ref, bkt_ref, v_ref, gc_ref, bonus_ref, lng_ref, lnb_ref, y_ref,
                h_ref, yc_ref, *, tt):
    n_chunks = tt // CHUNK

    @pl.when(pl.program_id(1) == 0)
    def _():
        h_ref[...] = jnp.zeros_like(h_ref)

    gi = _iota((2 * CHUNK, 2 * CHUNK), 0)
    gj = _iota((2 * CHUNK, 2 * CHUNK), 1) % CHUNK
    g_mask = gj <= jnp.where(gi < CHUNK, gi, gi - (CHUNK + 1))
    lane = _iota((CHUNK, PAIR), 1)
    low = lane < HEAD_DIM
    own = (low, jnp.logical_not(low))
    zeros_cp = jnp.zeros((CHUNK, PAIR), F32)
    cast = lambda t: t.astype(BF16)
    n_pairs = HEADS // 2

    def block(c, size):
        return pl.ds(c * size, size) if isinstance(c, int) else pl.ds(pl.multiple_of(c * size, size), size)

    def phase_a(c0):
        chunks = tuple(range(CHUNK_GROUP))
        cps = [(c, p) for c in chunks for p in range(n_pairs)]
        chs = [(c, h) for c in chunks for h in range(HEADS)]
        ld = lambda ref, c, p: ref[block(c0 + c, CHUNK), p * PAIR:(p + 1) * PAIR]
        rt = {(c, p): ld(rt_ref, c, p) for c, p in cps}
        at = {(c, p): ld(at_ref, c, p) for c, p in cps}
        vs = {(c, p): ld(v_ref, c, p) for c, p in cps}

        key_major = lambda ref, c, p: ref[block((c0 + c) * n_pairs + p, PAIR), :]
        bk = {(c, p): key_major(bk_ref, c, p) for c, p in cps}
        bkt = {(c, p): key_major(bkt_ref, c, p) for c, p in cps}

        g, x, vo, m = {}, {}, {}, {}
        for c, h in chs:
            p, s = divmod(h, 2)
            lhs = jnp.concatenate([jnp.where(own[s], rt[c, p], 0.0), jnp.where(own[s], at[c, p], 0.0)], axis=0)
            g[c, h] = cast(jnp.where(g_mask, _bf16_dot(cast(lhs), bk[c, p]), 0.0))
        for c, h in chs:
            p, s = divmod(h, 2)
            vo[c, h] = cast(jnp.where(own[s], 0.0, vs[c, p]))
            vak = _bf16_dot(g[c, h][CHUNK:, :], jnp.concatenate([cast(zeros_cp), vo[c, h]], axis=0))
            x[c, h] = jnp.where(own[s], at[c, p], 0.0) + vak
            m[c, h] = g[c, h][CHUNK:, :CHUNK]
        for step in range(5):
            for ch in chs:
                zz = _bf16_dot(m[ch], jnp.concatenate([cast(x[ch]), m[ch]], axis=1))
                x[ch] = x[ch] + zz[:, :PAIR]
                m[ch] = cast(zz[:, PAIR:])
        for ch in chs:
            x[ch] = x[ch] + _bf16_dot(m[ch], cast(x[ch]))
        o9 = {}
        for c, h in chs:
            p, s = divmod(h, 2)
            lhs = jnp.concatenate([g[c, h][:CHUNK, :], bkt[c, p][s * CHUNK:(s + 1) * CHUNK, :]], axis=0)
            o9[c, h] = _bf16_dot(lhs, jnp.concatenate([cast(x[c, h]), vo[c, h]], axis=0))
        out = {}
        for c, p in cps:
            t0, t1 = o9[c, 2 * p][:CHUNK], o9[c, 2 * p + 1][:CHUNK]
            b0, b1 = o9[c, 2 * p][CHUNK:], o9[c, 2 * p + 1][CHUNK:]
            r_p = rt[c, p] + jnp.where(low, t0, t1)
            y0 = jnp.where(low, t1, t0)
            phi = jnp.concatenate([jnp.where(low, b0, 0.0), jnp.where(low, 0.0, b1)], axis=0)
            hk = jnp.concatenate([jnp.where(low, 0.0, b0), jnp.where(low, b1, 0.0)], axis=0)
            out[c, p] = (cast(jnp.concatenate([r_p, phi], axis=0)), y0, hk)
        return out

    def phase_b(c0, c, prepared, states):
        new_states = []
        for p in range(n_pairs):
            lhs, y0, hk = prepared[c, p]
            gc = gc_ref[block(c0 + c, SUBLANES), p * PAIR:(p + 1) * PAIR][0:1, :]
            g_col = jnp.broadcast_to(gc, (PAIR, PAIR)).T
            zz = _bf16_dot(lhs, states[p])
            yc_ref[block(c0 + c, CHUNK), p * PAIR:(p + 1) * PAIR] = zz[:CHUNK] + y0
            new_states.append(g_col * states[p] + (zz[CHUNK:] + hk))
        return new_states

    states = [h_ref[p] for p in range(n_pairs)]
    for c0 in range(0, n_chunks, CHUNK_GROUP):
        prepared = phase_a(c0)
        for c in range(CHUNK_GROUP):
            states = phase_b(c0, c, prepared, states)
    for p in range(n_pairs):
        h_ref[p] = states[p]

    ones_bd = _seg_matrix(False)
    y = yc_ref[...]
    mean = _seg_sum(y, ones_bd) * (1.0 / HEAD_DIM)
    yc = y - mean
    var = _seg_sum(yc * yc, ones_bd) * (1.0 / HEAD_DIM)
    yn = yc * lax.rsqrt(var + LNX_EPS) * lng_ref[...] + lnb_ref[...]
    y_ref[...] = _swap_halves(yn + bonus_ref[...] * v_ref[...])


def _wkv(rt, at, bk, bkt, v_sw, gc, bonus_sw, lnx_g_sw, lnx_b_sw, bsz, seq, tt):
    n_t = seq // tt
    const = lambda b, i: (0, 0)
    tile = lambda rows, w: pl.BlockSpec((rows, w), lambda b, i: (b * n_t + i, 0))
    pair_rows = tt // CHUNK * (HEADS // 2) * PAIR
    return pl.pallas_call(
        functools.partial(_wkv_kernel, tt=tt),
        grid=(bsz, n_t),
        in_specs=[tile(tt, RWKV_WIDTH), tile(tt, RWKV_WIDTH), tile(pair_rows, PAIR), tile(pair_rows, PAIR),
                  tile(tt, RWKV_WIDTH), tile(tt // CHUNK * SUBLANES, RWKV_WIDTH), tile(tt, RWKV_WIDTH),
                  pl.BlockSpec((1, RWKV_WIDTH), const), pl.BlockSpec((1, RWKV_WIDTH), const)],
        out_specs=tile(tt, RWKV_WIDTH),
        out_shape=jax.ShapeDtypeStruct((bsz * seq, RWKV_WIDTH), F32),
        scratch_shapes=[pltpu.VMEM((HEADS // 2, PAIR, PAIR), F32),
                        pltpu.VMEM((tt, RWKV_WIDTH), F32)],
        compiler_params=pltpu.CompilerParams(dimension_semantics=("arbitrary", "arbitrary"),
                                             vmem_limit_bytes=VMEM_LIMIT_BYTES),
        name="wkv7",
    )(rt, at, bk, bkt, v_sw, gc, bonus_sw, lnx_g_sw, lnx_b_sw)


def _s5_kernel(u_ref, w_ref, cm_ref, lam2_ref, d_ref, y_ref, ut_ref, prev_ref, carry_ref, st_ref, *, rows):
    @pl.when(pl.program_id(0) == 0)
    def _():
        prev_ref[...] = jnp.zeros_like(prev_ref)
        carry_ref[...] = jnp.zeros_like(carry_ref)

    frames = rows // S5_BATCH
    n_col = S5_WIDTH // LANES
    for b in range(S5_BATCH):
        for j in range(n_col):
            col = b * S5_WIDTH + j * LANES
            ut_ref[j, pl.ds(b, frames, stride=S5_BATCH), :] = u_ref[:, col:col + LANES]
    u = jnp.concatenate([ut_ref[j] for j in range(n_col)], axis=1)
    hs = S5_HALF_STATE
    shifted = pltpu.roll(u, S5_BATCH, 0)
    row8 = _iota((SCAN_ROWS, S5_WIDTH), 0)
    head = jnp.where(row8 < S5_BATCH, pltpu.roll(prev_ref[...], S5_BATCH, 0), shifted[:SCAN_ROWS])
    u_prev = jnp.concatenate([head, shifted[SCAN_ROWS:]], axis=0)
    prev_ref[...] = u[rows - SCAN_ROWS:, :]
    ub = u.astype(BF16)
    upb = u_prev.astype(BF16)
    for half in range(S5_PARTS):
        cols = slice(half * S5_HALF, (half + 1) * S5_HALF)
        st_ref[:, half * 2 * hs:(half + 1) * 2 * hs] = jnp.dot(
            jnp.concatenate([ub[:, cols], upb[:, cols]], axis=1), w_ref[half], preferred_element_type=F32)

    n_blocks = rows // SCAN_ROWS
    for half in range(S5_PARTS):
        re_l = pl.ds(half * 2 * hs, hs)
        im_l = pl.ds(half * 2 * hs + hs, hs)
        l_re = lam2_ref[:, re_l]
        l_im = lam2_ref[:, im_l]

        def block(i, carry, re_l=re_l, im_l=im_l, l_re=l_re, l_im=l_im):
            c_re, c_im = carry
            blk = pl.ds(i * SCAN_ROWS, SCAN_ROWS)
            x_re = st_ref[blk, re_l] + (l_re * c_re - l_im * c_im)
            x_im = st_ref[blk, im_l] + (l_re * c_im + l_im * c_re)
            st_ref[blk, re_l] = x_re
            st_ref[blk, im_l] = x_im
            return x_re, x_im

        carry = (carry_ref[:, re_l], carry_ref[:, im_l])
        for i in range(n_blocks):
            carry = block(i, carry)
        c_re, c_im = carry
        carry_ref[:, re_l] = c_re
        carry_ref[:, im_l] = c_im

    outs = [jnp.dot(st_ref[:, half * 2 * hs:(half + 1) * 2 * hs].astype(BF16), cm_ref[half],
                    preferred_element_type=F32) for half in range(S5_PARTS)]
    y = jnp.concatenate(outs, axis=1) + d_ref[...] * u
    for j in range(n_col):
        ut_ref[j] = y[:, j * LANES:(j + 1) * LANES]
    for b in range(S5_BATCH):
        for j in range(n_col):
            col = b * S5_WIDTH + j * LANES
            y_ref[:, col:col + LANES] = ut_ref[j, pl.ds(b, frames, stride=S5_BATCH), :]


def _s5(u_lb, wcat, cmat, lam2, d_row, rows):
    seq = u_lb.shape[0]
    frames = rows // S5_BATCH
    hs = S5_HALF_STATE
    const2 = lambda i: (0, 0)
    const3 = lambda i: (0, 0, 0)
    return pl.pallas_call(
        functools.partial(_s5_kernel, rows=rows),
        grid=(seq // frames,),
        in_specs=[pl.BlockSpec((frames, S5_BATCH * S5_WIDTH), lambda i: (i, 0)),
                  pl.BlockSpec((S5_PARTS, 2 * S5_HALF, 2 * hs), const3),
                  pl.BlockSpec((S5_PARTS, 2 * hs, S5_HALF), const3),
                  pl.BlockSpec((SCAN_ROWS, 2 * S5_PARTS * hs), const2),
                  pl.BlockSpec((1, S5_WIDTH), const2)],
        out_specs=pl.BlockSpec((frames, S5_BATCH * S5_WIDTH), lambda i: (i, 0)),
        out_shape=jax.ShapeDtypeStruct((seq, S5_BATCH * S5_WIDTH), F32),
        scratch_shapes=[pltpu.VMEM((S5_WIDTH // LANES, rows, LANES), F32),
                        pltpu.VMEM((SCAN_ROWS, S5_WIDTH), F32),
                        pltpu.VMEM((SCAN_ROWS, 2 * S5_PARTS * hs), F32),
                        pltpu.VMEM((rows, 2 * S5_PARTS * hs), F32)],
        compiler_params=pltpu.CompilerParams(dimension_semantics=("arbitrary",),
                                             vmem_limit_bytes=VMEM_LIMIT_BYTES),
        name="s5",
    )(u_lb, wcat, cmat, lam2, d_row)


def _s5_params(lam_re, lam_im, log_dt, b_re, b_im, c_re, c_im):
    dt = jnp.exp(log_dt)[:, None]
    mag = jnp.exp(lam_re * dt)
    ang = lam_im * dt
    ab_re = mag * jnp.cos(ang)
    ab_im = mag * jnp.sin(ang)
    den = lam_re * lam_re + lam_im * lam_im
    nr = ab_re - 1.0
    f_re = (nr * lam_re + ab_im * lam_im) / den
    f_im = (ab_im * lam_re - nr * lam_im) / den
    bb_re = f_re[..., None] * b_re - f_im[..., None] * b_im
    bb_im = f_re[..., None] * b_im + f_im[..., None] * b_re

    gh = S5_GROUPS // S5_PARTS
    eye = jnp.eye(gh, dtype=F32)

    def in_half(bb):
        return jnp.einsum('gph,gk->ghkp', bb, eye).reshape(gh * S5_GROUP_DIM, gh * S5_STATE)

    def out_half(cc):
        return jnp.einsum('ghp,gk->gpkh', cc, eye).reshape(gh * S5_STATE, gh * S5_GROUP_DIM)

    lb_re = ab_re[..., None] * bb_re - ab_im[..., None] * bb_im
    lb_im = ab_re[..., None] * bb_im + ab_im[..., None] * bb_re
    halves = [slice(i * gh, (i + 1) * gh) for i in range(S5_PARTS)]
    wcat = jnp.stack([jnp.concatenate(
        [jnp.concatenate([in_half(bb_re[s]), in_half(bb_im[s])], axis=1),
         jnp.concatenate([in_half(lb_re[s]), in_half(lb_im[s])], axis=1)], axis=0)
        for s in halves])
    cmat = jnp.stack([jnp.concatenate([out_half(c_re[s]), -out_half(c_im[s])], axis=0)
                      for s in halves])

    l2_re = (mag * mag * jnp.cos(2.0 * ang)).reshape(S5_PARTS, gh * S5_STATE)
    l2_im = (mag * mag * jnp.sin(2.0 * ang)).reshape(S5_PARTS, gh * S5_STATE)
    lam2 = jnp.concatenate([part for i in range(S5_PARTS) for part in (l2_re[i], l2_im[i])])[None, :]
    lam2 = jnp.broadcast_to(lam2, (SCAN_ROWS, 2 * S5_GROUPS * S5_STATE))
    return wcat.astype(BF16), cmat.astype(BF16), lam2


def _out_kernel(x_ref, ya_ref, ga_ref, ys_ref, gb_ref, m_ref, wglu_ref, bglu_ref, pa_ref, pb_ref,
                wout_ref, fg_ref, o_ref):
    y_a = ya_ref[...] * jax.nn.silu(ga_ref[...].astype(F32))
    proj_a = jnp.dot(y_a.astype(BF16), pa_ref[...], preferred_element_type=F32)
    ys = jax.nn.gelu(ys_ref[...])
    glu = jnp.dot(ys.astype(BF16), wglu_ref[...], preferred_element_type=F32) + bglu_ref[...]
    y_b = glu[:, :S5_WIDTH] * jax.nn.sigmoid(glu[:, S5_WIDTH:]) * jax.nn.silu(gb_ref[...].astype(F32))
    proj_b = jnp.dot(y_b.astype(BF16), pb_ref[...], preferred_element_type=F32)
    m = m_ref[...].astype(F32)
    merged = jax.nn.sigmoid(m[:, :D_MODEL]) * proj_a + jax.nn.sigmoid(m[:, D_MODEL:]) * proj_b
    x_new = x_ref[...] + jnp.dot(merged.astype(BF16), wout_ref[...], preferred_element_type=F32)
    o_ref[...] = _rms_norm(x_new, fg_ref[...])


def _out(x2, ya, ga, ys_lb, gb, m, w_glu, b_glu, p_a, p_b, w_out, final_g, tm):
    t = x2.shape[0]
    const = lambda i: (0, 0)
    tile = lambda w: pl.BlockSpec((tm, w), lambda i: (i, 0))
    full = lambda a: pl.BlockSpec(a.shape, const)
    return pl.pallas_call(
        _out_kernel,
        grid=(t // tm,),
        in_specs=[tile(D_MODEL), tile(RWKV_WIDTH), tile(RWKV_WIDTH),
                  _frame_batch_spec(tm, ys_lb.shape[0] // tm), tile(S5_WIDTH),
                  tile(2 * D_MODEL), full(w_glu), full(b_glu), full(p_a), full(p_b), full(w_out),
                  full(final_g)],
        out_specs=tile(D_MODEL),
        out_shape=jax.ShapeDtypeStruct((t, D_MODEL), F32),
        compiler_params=pltpu.CompilerParams(dimension_semantics=("arbitrary",),
                                             vmem_limit_bytes=VMEM_LIMIT_BYTES),
        name="merge_out",
    )(x2, ya, ga, ys_lb, gb, m, w_glu, b_glu, p_a, p_b, w_out, final_g)


def kernel(x, norm_g, w_in, mu_shift, w0, w_up, a0, a_up, k_k, k_a, r_k, lnx_g, lnx_b, lam_re, lam_im,
           log_dt, b_re, b_im, c_re, c_im, d_skip, w_glu, b_glu, p_a, p_b, w_out, final_g):
    bsz, seq, _ = x.shape
    assert norm_g.shape[0] == 1, "single-layer block: the final RMSNorm is fused into the output kernel"
    tm = min(512, seq)
    tt_wkv = min(256, seq)
    assert bsz == S5_BATCH, "the S5 kernel packs 2 frames x 4 batch rows per sublane block"
    rows_s5 = min(1024, seq * bsz)
    x2 = x.reshape(bsz * seq, D_MODEL)
    row = lambda p: p.reshape(1, -1).astype(F32)
    l = 0
    (rt, at, bk, bkt, v_sw, gc, bonus_sw, ga, u_lb, gb, m) = _prep(
        x2, row(norm_g[l]), w_in[l].astype(BF16), row(mu_shift[l]), row(w0[l]), w_up[l], row(a0[l]), a_up[l],
        row(k_k[l]), row(k_a[l]), row(r_k[l]), tt_wkv, bsz, seq)
    swap_row = lambda p: row(p).reshape(HEADS // 2, 2, HEAD_DIM)[:, ::-1].reshape(1, RWKV_WIDTH)
    ya = _wkv(rt, at, bk, bkt, v_sw, gc, bonus_sw, swap_row(lnx_g[l]), swap_row(lnx_b[l]), bsz, seq,
              min(2 * tt_wkv, seq))
    wcat, cmat, lam2 = _s5_params(lam_re[l], lam_im[l], log_dt[l], b_re[l], b_im[l], c_re[l], c_im[l])
    ys_lb = _s5(u_lb, wcat, cmat, lam2, row(d_skip[l]), rows_s5)
    out = _out(x2, ya, ga, ys_lb, gb, m, w_glu[l].astype(BF16), row(b_glu[l]), p_a[l].astype(BF16),
               p_b[l].astype(BF16), w_out[l].astype(BF16), row(final_g), tm)
    return out.reshape(bsz, seq, D_MODEL)
```

```python
import functools
import math

import jax
import jax.numpy as jnp
from jax import lax
from jax.experimental import pallas as pl
from jax.experimental.pallas import tpu as pltpu

F32 = jnp.float32
BF16 = jnp.bfloat16

D_MODEL = 1024
HEADS = 16
HEAD_DIM = 64
RWKV_WIDTH = HEADS * HEAD_DIM
LORA = 64
S5_GROUPS = 32
S5_GROUP_DIM = 16
S5_WIDTH = S5_GROUPS * S5_GROUP_DIM
S5_STATE = 64
SHIFT_WIDTH = 3 * RWKV_WIDTH + 2 * LORA
IN_WIDTH = SHIFT_WIDTH + RWKV_WIDTH + 2 * S5_WIDTH + 2 * D_MODEL
RMS_EPS = 1e-6
LNX_EPS = 64e-5

LANES = 128
SUBLANES = 8
CHUNK = 64
PAIR = 2 * HEAD_DIM
CHUNK_GROUP = 2
PROJ_SLICE = 512
SEG_TILE = 256
S5_PARTS = 4
S5_HALF = S5_WIDTH // S5_PARTS
S5_HALF_STATE = (S5_GROUPS // S5_PARTS) * S5_STATE
SCAN_ROWS = SUBLANES
S5_BATCH = 4

VMEM_LIMIT_BYTES = 56 * 1024 * 1024


def _split_bf16(x):
    hi = x.astype(BF16)
    lo = (x - hi.astype(F32)).astype(BF16)
    return hi, lo


def _bf16_dot(a, b):
    return jnp.dot(a.astype(BF16), b.astype(BF16), preferred_element_type=F32)


def _iota(shape, dim):
    return lax.broadcasted_iota(jnp.int32, shape, dim)


def _seg_sum(x, ones_bd):
    xb = x.astype(BF16)
    cols = [jnp.dot(xb[:, j * SEG_TILE:(j + 1) * SEG_TILE], ones_bd, preferred_element_type=F32)
            for j in range(x.shape[1] // SEG_TILE)]
    return jnp.concatenate(cols, axis=1)


def _swap_halves(x):
    n = x.shape[1]
    low = (_iota(x.shape, 1) % PAIR) < HEAD_DIM
    return jnp.where(low, pltpu.roll(x, n - HEAD_DIM, 1), pltpu.roll(x, HEAD_DIM, 1))


def _rms_norm(x, g):
    return x * lax.rsqrt(jnp.mean(x * x, axis=-1, keepdims=True) + RMS_EPS) * g


_IN_SPLITS = (0, SHIFT_WIDTH, SHIFT_WIDTH + RWKV_WIDTH, SHIFT_WIDTH + RWKV_WIDTH + S5_WIDTH,
              SHIFT_WIDTH + RWKV_WIDTH + 2 * S5_WIDTH, IN_WIDTH)


def _seg_matrix(swapped):
    seg_i = _iota((SEG_TILE, SEG_TILE), 0) // HEAD_DIM
    seg_j = _iota((SEG_TILE, SEG_TILE), 1) // HEAD_DIM
    if swapped:
        seg_j = seg_j + 1 - 2 * (seg_j % 2)
    return jnp.where(seg_i == seg_j, 1.0, 0.0).astype(BF16)


def _prep_kernel(x_ref, g_ref, w_ref, mu_ref, w0_ref, wup_ref, a0_ref, aup_ref, kk_ref, ka_ref, rk_ref,
                 rt_ref, at_ref, bk_ref, v_ref, gc_ref, bonus_ref, ga_ref, u_ref, gb_ref, m_ref,
                 prev_ref, *, tm, n_t):
    n_chunks = tm // CHUNK

    @pl.when(pl.program_id(0) % n_t == 0)
    def _():
        prev_ref[...] = jnp.zeros_like(prev_ref)

    h = _rms_norm(x_ref[...], g_ref[...]).astype(BF16)
    proj = lambda lo, hi: jnp.dot(h, w_ref[:, lo:hi], preferred_element_type=F32)

    out_cols = []
    for o_ref, lo, hi in zip((ga_ref, u_ref, gb_ref, m_ref), _IN_SPLITS[1:-1], _IN_SPLITS[2:]):
        for c0 in range(lo, hi, PROJ_SLICE):
            out_cols.append((o_ref, c0 - lo, c0, min(c0 + PROJ_SLICE, hi)))
    out_cols = iter(out_cols)

    def project(n=1):
        for _ in range(n):
            item = next(out_cols, None)
            if item is not None:
                o_ref, at_col, lo, hi = item
                o_ref[:, at_col:at_col + hi - lo] = proj(lo, hi).astype(o_ref.dtype)

    row0 = _iota((tm, 1), 0) == 0

    def shifted(z, lo, hi):
        z_prev = jnp.where(row0, prev_ref[:, lo:hi], pltpu.roll(z, 1, 0))
        return z + mu_ref[:, lo:hi] * (z_prev - z), z[tm - 1:tm, :]

    x_lo = 3 * RWKV_WIDTH
    zx, last_x = shifted(proj(x_lo, SHIFT_WIDTH), x_lo, SHIFT_WIDTH)
    zk_raw = proj(RWKV_WIDTH, 2 * RWKV_WIDTH)
    wl = w0_ref[...] + _bf16_dot(jnp.tanh(zx[:, :LORA]), wup_ref[...])
    logw = -math.exp(-0.5) * jax.nn.sigmoid(wl)
    iclr = jax.nn.sigmoid(a0_ref[...] + _bf16_dot(zx[:, LORA:], aup_ref[...]))
    zr_raw = proj(0, RWKV_WIDTH)
    ti = _iota((tm, tm), 0)
    si = _iota((tm, tm), 1)
    tril_bd = jnp.where(((ti // CHUNK) == (si // CHUNK)) & (si <= ti), 1.0, 0.0).astype(BF16)
    lw_hi, lw_lo = _split_bf16(logw)
    cum = (jnp.dot(tril_bd, lw_hi, preferred_element_type=F32)
           + jnp.dot(tril_bd, lw_lo, preferred_element_type=F32))
    k, last_k = shifted(zk_raw, RWKV_WIDTH, 2 * RWKV_WIDTH)
    zv_raw = proj(2 * RWKV_WIDTH, 3 * RWKV_WIDTH)
    kk = k * kk_ref[...]
    kk = kk * lax.rsqrt(jnp.maximum(_seg_sum(kk * kk, _seg_matrix(False)), 1e-24))
    k_h = k * (1.0 + (iclr - 1.0) * ka_ref[...])
    b = kk * iclr
    project(2)
    by_chunk = lambda t: t.reshape(n_chunks, CHUNK, RWKV_WIDTH)
    g_c = jnp.exp(by_chunk(cum)[:, CHUNK - 1:CHUNK, :])
    gc_ref[...] = jnp.broadcast_to(g_c, (n_chunks, SUBLANES, RWKV_WIDTH)).reshape(n_chunks * SUBLANES, RWKV_WIDTH)
    r, last_r = shifted(zr_raw, 0, RWKV_WIDTH)
    rt_ref[...] = r * jnp.exp(cum)
    project(2)
    at_ref[...] = -kk * jnp.exp(cum - logw)
    bonus_ref[...] = _seg_sum(r * k_h * rk_ref[...], _seg_matrix(True))
    project(2)
    v, last_v = shifted(zv_raw, 2 * RWKV_WIDTH, 3 * RWKV_WIDTH)
    v_ref[...] = _swap_halves(v)
    prev_ref[...] = jnp.concatenate([last_r, last_k, last_v, last_x], axis=1)
    project(2)
    g_inv = jnp.exp(-cum)
    b_t = b * g_inv
    k_t = k_h * g_inv
    project(2)
    for c in range(n_chunks):
        for p in range(HEADS // 2):
            blk = (slice(c * CHUNK, (c + 1) * CHUNK), slice(p * PAIR, (p + 1) * PAIR))
            dst = slice((c * (HEADS // 2) + p) * PAIR, (c * (HEADS // 2) + p + 1) * PAIR)
            bk_ref[dst, :] = jnp.concatenate([b_t[blk], k_t[blk]], axis=0).T.astype(BF16)
        project(1)
    project(64)


def _frame_batch_spec(tm, n_t):
    return pl.BlockSpec((tm, S5_WIDTH), lambda i: (i % n_t, i // n_t))


def _prep(x2, norm_g, w_in_bf16, mu, w0, w_up, a0, a_up, k_k, k_a, r_k, tm, bsz, seq):
    t = x2.shape[0]
    n_t = seq // tm
    const = lambda i: (0, 0)
    rowspec = lambda w: pl.BlockSpec((1, w), const)
    tile = lambda rows, w: pl.BlockSpec((rows, w), lambda i: (i, 0))
    pair_rows = tm // CHUNK * (HEADS // 2) * PAIR
    wide = jax.ShapeDtypeStruct((t, RWKV_WIDTH), F32)
    key_major = jax.ShapeDtypeStruct((t // CHUNK * (HEADS // 2) * PAIR, PAIR), BF16)
    out_shape = [wide, wide, key_major, wide,
                 jax.ShapeDtypeStruct((t // CHUNK * SUBLANES, RWKV_WIDTH), F32), wide,
                 jax.ShapeDtypeStruct((t, RWKV_WIDTH), BF16),
                 jax.ShapeDtypeStruct((seq, bsz * S5_WIDTH), F32),
                 jax.ShapeDtypeStruct((t, S5_WIDTH), BF16),
                 jax.ShapeDtypeStruct((t, 2 * D_MODEL), BF16)]
    out_specs = [tile(tm, RWKV_WIDTH), tile(tm, RWKV_WIDTH), tile(pair_rows, PAIR),
                 tile(tm, RWKV_WIDTH), tile(tm // CHUNK * SUBLANES, RWKV_WIDTH), tile(tm, RWKV_WIDTH),
                 tile(tm, RWKV_WIDTH), _frame_batch_spec(tm, n_t), tile(tm, S5_WIDTH), tile(tm, 2 * D_MODEL)]
    return pl.pallas_call(
        functools.partial(_prep_kernel, tm=tm, n_t=n_t),
        grid=(t // tm,),
        in_specs=[pl.BlockSpec((tm, D_MODEL), lambda i: (i, 0)),
                  rowspec(D_MODEL),
                  pl.BlockSpec((D_MODEL, IN_WIDTH), const, pipeline_mode=pl.Buffered(1)),
                  rowspec(SHIFT_WIDTH), rowspec(RWKV_WIDTH),
                  pl.BlockSpec((LORA, RWKV_WIDTH), const),
                  rowspec(RWKV_WIDTH),
                  pl.BlockSpec((LORA, RWKV_WIDTH), const),
                  rowspec(RWKV_WIDTH), rowspec(RWKV_WIDTH), rowspec(RWKV_WIDTH)],
        out_specs=out_specs,
        out_shape=out_shape,
        scratch_shapes=[pltpu.VMEM((1, SHIFT_WIDTH), F32)],
        compiler_params=pltpu.CompilerParams(dimension_semantics=("arbitrary",),
                                             vmem_limit_bytes=VMEM_LIMIT_BYTES),
        name="inproj_prep",
    )(x2, norm_g, w_in_bf16, mu, w0, w_up, a0, a_up, k_k, k_a, r_k)


def _wkv_kernel(rt_ref, at_ref, bk_ref, v_ref, gc_ref, bonus_ref, lng_ref, lnb_ref, y_ref,
                h_ref, yc_ref, *, tt):
    n_chunks = tt // CHUNK

    @pl.when(pl.program_id(1) == 0)
    def _():
        h_ref[...] = jnp.zeros_like(h_ref)

    gi = _iota((2 * CHUNK, 2 * CHUNK), 0)
    gj = _iota((2 * CHUNK, 2 * CHUNK), 1) % CHUNK
    g_mask = gj <= jnp.where(gi < CHUNK, gi, gi - (CHUNK + 1))
    lane = _iota((CHUNK, PAIR), 1)
    low = lane < HEAD_DIM
    own = (low, jnp.logical_not(low))
    zeros_cp = jnp.zeros((CHUNK, PAIR), F32)
    cast = lambda t: t.astype(BF16)
    n_pairs = HEADS // 2

    def block(c, size):
        return pl.ds(c * size, size) if isinstance(c, int) else pl.ds(pl.multiple_of(c * size, size), size)

    def phase_a(c0):
        chunks = tuple(range(CHUNK_GROUP))
        cps = [(c, p) for c in chunks for p in range(n_pairs)]
        chs = [(c, h) for c in chunks for h in range(HEADS)]
        ld = lambda ref, c, p: ref[block(c0 + c, CHUNK), p * PAIR:(p + 1) * PAIR]
        rt = {(c, p): ld(rt_ref, c, p) for c, p in cps}
        at = {(c, p): ld(at_ref, c, p) for c, p in cps}
        vs = {(c, p): ld(v_ref, c, p) for c, p in cps}

        key_major = lambda ref, c, p: ref[block((c0 + c) * n_pairs + p, PAIR), :]
        bk = {(c, p): key_major(bk_ref, c, p) for c, p in cps}

        g, x, vo, m = {}, {}, {}, {}
        for c, h in chs:
            p, s = divmod(h, 2)
            lhs = jnp.concatenate([jnp.where(own[s], rt[c, p], 0.0), jnp.where(own[s], at[c, p], 0.0)], axis=0)
            g[c, h] = cast(jnp.where(g_mask, _bf16_dot(cast(lhs), bk[c, p]), 0.0))
        for c, h in chs:
            p, s = divmod(h, 2)
            vo[c, h] = cast(jnp.where(own[s], 0.0, vs[c, p]))
            vak = _bf16_dot(g[c, h][CHUNK:, :], jnp.concatenate([cast(zeros_cp), vo[c, h]], axis=0))
            x[c, h] = jnp.where(own[s], at[c, p], 0.0) + vak
            m[c, h] = g[c, h][CHUNK:, :CHUNK]
        for step in range(5):
            for ch in chs:
                zz = _bf16_dot(m[ch], jnp.concatenate([cast(x[ch]), m[ch]], axis=1))
                x[ch] = x[ch] + zz[:, :PAIR]
                m[ch] = cast(zz[:, PAIR:])
        for ch in chs:
            x[ch] = x[ch] + _bf16_dot(m[ch], cast(x[ch]))
        o9 = {}
        for c, h in chs:
            p, s = divmod(h, 2)
            lhs = jnp.concatenate([g[c, h][:CHUNK, :], bk[c, p][s * CHUNK:(s + 1) * CHUNK, :]], axis=0)
            o9[c, h] = _bf16_dot(lhs, jnp.concatenate([cast(x[c, h]), vo[c, h]], axis=0))
        out = {}
        for c, p in cps:
            t0, t1 = o9[c, 2 * p][:CHUNK], o9[c, 2 * p + 1][:CHUNK]
            b0, b1 = o9[c, 2 * p][CHUNK:], o9[c, 2 * p + 1][CHUNK:]
            r_p = rt[c, p] + jnp.where(low, t0, t1)
            y0 = jnp.where(low, t1, t0)
            phi = jnp.concatenate([jnp.where(low, b0, 0.0), jnp.where(low, 0.0, b1)], axis=0)
            hk = jnp.concatenate([jnp.where(low, 0.0, b0), jnp.where(low, b1, 0.0)], axis=0)
            out[c, p] = (cast(jnp.concatenate([r_p, phi], axis=0)), y0, hk)
        return out

    def phase_b(c0, c, prepared, states):
        new_states = []
        for p in range(n_pairs):
            lhs, y0, hk = prepared[c, p]
            gc = gc_ref[block(c0 + c, SUBLANES), p * PAIR:(p + 1) * PAIR][0:1, :]
            g_col = jnp.broadcast_to(gc, (PAIR, PAIR)).T
            zz = _bf16_dot(lhs, states[p])
            yc_ref[block(c0 + c, CHUNK), p * PAIR:(p + 1) * PAIR] = zz[:CHUNK] + y0
            new_states.append(g_col * (states[p] + (zz[CHUNK:] + hk)))
        return new_states

    states = [h_ref[p] for p in range(n_pairs)]
    for c0 in range(0, n_chunks, CHUNK_GROUP):
        prepared = phase_a(c0)
        for c in range(CHUNK_GROUP):
            states = phase_b(c0, c, prepared, states)
    for p in range(n_pairs):
        h_ref[p] = states[p]

    ones_bd = _seg_matrix(False)
    y = yc_ref[...]
    mean = _seg_sum(y, ones_bd) * (1.0 / HEAD_DIM)
    yc = y - mean
    var = _seg_sum(yc * yc, ones_bd) * (1.0 / HEAD_DIM)
    yn = yc * lax.rsqrt(var + LNX_EPS) * lng_ref[...] + lnb_ref[...]
    y_ref[...] = _swap_halves(yn + bonus_ref[...] * v_ref[...])


def _wkv(rt, at, bk, v_sw, gc, bonus_sw, lnx_g_sw, lnx_b_sw, bsz, seq, tt):
    n_t = seq // tt
    const = lambda b, i: (0, 0)
    tile = lambda rows, w: pl.BlockSpec((rows, w), lambda b, i: (b * n_t + i, 0))
    pair_rows = tt // CHUNK * (HEADS // 2) * PAIR
    return pl.pallas_call(
        functools.partial(_wkv_kernel, tt=tt),
        grid=(bsz, n_t),
        in_specs=[tile(tt, RWKV_WIDTH), tile(tt, RWKV_WIDTH), tile(pair_rows, PAIR),
                  tile(tt, RWKV_WIDTH), tile(tt // CHUNK * SUBLANES, RWKV_WIDTH), tile(tt, RWKV_WIDTH),
                  pl.BlockSpec((1, RWKV_WIDTH), const), pl.BlockSpec((1, RWKV_WIDTH), const)],
        out_specs=tile(tt, RWKV_WIDTH),
        out_shape=jax.ShapeDtypeStruct((bsz * seq, RWKV_WIDTH), F32),
        scratch_shapes=[pltpu.VMEM((HEADS // 2, PAIR, PAIR), F32),
                        pltpu.VMEM((tt, RWKV_WIDTH), F32)],
        compiler_params=pltpu.CompilerParams(dimension_semantics=("arbitrary", "arbitrary"),
                                             vmem_limit_bytes=VMEM_LIMIT_BYTES),
        name="wkv7",
    )(rt, at, bk, v_sw, gc, bonus_sw, lnx_g_sw, lnx_b_sw)


def _s5_kernel(u_ref, w_ref, cm_ref, lam2_ref, d_ref, y_ref, ut_ref, prev_ref, carry_ref, st_ref, *, rows):
    @pl.when(pl.program_id(0) == 0)
    def _():
        prev_ref[...] = jnp.zeros_like(prev_ref)
        carry_ref[...] = jnp.zeros_like(carry_ref)

    frames = rows // S5_BATCH
    n_col = S5_WIDTH // LANES
    for b in range(S5_BATCH):
        for j in range(n_col):
            col = b * S5_WIDTH + j * LANES
            ut_ref[j, pl.ds(b, frames, stride=S5_BATCH), :] = u_ref[:, col:col + LANES]
    u = jnp.concatenate([ut_ref[j] for j in range(n_col)], axis=1)
    hs = S5_HALF_STATE
    shifted = pltpu.roll(u, S5_BATCH, 0)
    row8 = _iota((SCAN_ROWS, S5_WIDTH), 0)
    head = jnp.where(row8 < S5_BATCH, pltpu.roll(prev_ref[...], S5_BATCH, 0), shifted[:SCAN_ROWS])
    u_prev = jnp.concatenate([head, shifted[SCAN_ROWS:]], axis=0)
    prev_ref[...] = u[rows - SCAN_ROWS:, :]
    ub = u.astype(BF16)
    upb = u_prev.astype(BF16)
    for half in range(S5_PARTS):
        cols = slice(half * S5_HALF, (half + 1) * S5_HALF)
        st_ref[:, half * 2 * hs:(half + 1) * 2 * hs] = jnp.dot(
            jnp.concatenate([ub[:, cols], upb[:, cols]], axis=1), w_ref[half], preferred_element_type=F32)

    n_blocks = rows // SCAN_ROWS
    for half in range(S5_PARTS):
        re_l = pl.ds(half * 2 * hs, hs)
        im_l = pl.ds(half * 2 * hs + hs, hs)
        l_re = lam2_ref[:, re_l]
        l_im = lam2_ref[:, im_l]

        def block(i, carry, re_l=re_l, im_l=im_l, l_re=l_re, l_im=l_im):
            c_re, c_im = carry
            blk = pl.ds(i * SCAN_ROWS, SCAN_ROWS)
            x_re = st_ref[blk, re_l] + (l_re * c_re - l_im * c_im)
            x_im = st_ref[blk, im_l] + (l_re * c_im + l_im * c_re)
            st_ref[blk, re_l] = x_re
            st_ref[blk, im_l] = x_im
            return x_re, x_im

        carry = (carry_ref[:, re_l], carry_ref[:, im_l])
        for i in range(n_blocks):
            carry = block(i, carry)
        c_re, c_im = carry
        carry_ref[:, re_l] = c_re
        carry_ref[:, im_l] = c_im

    outs = [jnp.dot(st_ref[:, half * 2 * hs:(half + 1) * 2 * hs].astype(BF16), cm_ref[half],
                    preferred_element_type=F32) for half in range(S5_PARTS)]
    y = jnp.concatenate(outs, axis=1) + d_ref[...] * u
    for j in range(n_col):
        ut_ref[j] = y[:, j * LANES:(j + 1) * LANES]
    for b in range(S5_BATCH):
        for j in range(n_col):
            col = b * S5_WIDTH + j * LANES
            y_ref[:, col:col + LANES] = ut_ref[j, pl.ds(b, frames, stride=S5_BATCH), :]


def _s5(u_lb, wcat, cmat, lam2, d_row, rows):
    seq = u_lb.shape[0]
    frames = rows // S5_BATCH
    hs = S5_HALF_STATE
    const2 = lambda i: (0, 0)
    const3 = lambda i: (0, 0, 0)
    return pl.pallas_call(
        functools.partial(_s5_kernel, rows=rows),
        grid=(seq // frames,),
        in_specs=[pl.BlockSpec((frames, S5_BATCH * S5_WIDTH), lambda i: (i, 0)),
                  pl.BlockSpec((S5_PARTS, 2 * S5_HALF, 2 * hs), const3),
                  pl.BlockSpec((S5_PARTS, 2 * hs, S5_HALF), const3),
                  pl.BlockSpec((SCAN_ROWS, 2 * S5_PARTS * hs), const2),
                  pl.BlockSpec((1, S5_WIDTH), const2)],
        out_specs=pl.BlockSpec((frames, S5_BATCH * S5_WIDTH), lambda i: (i, 0)),
        out_shape=jax.ShapeDtypeStruct((seq, S5_BATCH * S5_WIDTH), F32),
        scratch_shapes=[pltpu.VMEM((S5_WIDTH // LANES, rows, LANES), F32),
                        pltpu.VMEM((SCAN_ROWS, S5_WIDTH), F32),
                        pltpu.VMEM((SCAN_ROWS, 2 * S5_PARTS * hs), F32),
                        pltpu.VMEM((rows, 2 * S5_PARTS * hs), F32)],
        compiler_params=pltpu.CompilerParams(dimension_semantics=("arbitrary",),
                                             vmem_limit_bytes=VMEM_LIMIT_BYTES),
        name="s5",
    )(u_lb, wcat, cmat, lam2, d_row)


def _s5_params(lam_re, lam_im, log_dt, b_re, b_im, c_re, c_im):
    dt = jnp.exp(log_dt)[:, None]
    mag = jnp.exp(lam_re * dt)
    ang = lam_im * dt
    ab_re = mag * jnp.cos(ang)
    ab_im = mag * jnp.sin(ang)
    den = lam_re * lam_re + lam_im * lam_im
    nr = ab_re - 1.0
    f_re = (nr * lam_re + ab_im * lam_im) / den
    f_im = (ab_im * lam_re - nr * lam_im) / den
    bb_re = f_re[..., None] * b_re - f_im[..., None] * b_im
    bb_im = f_re[..., None] * b_im + f_im[..., None] * b_re

    gh = S5_GROUPS // S5_PARTS
    eye = jnp.eye(gh, dtype=F32)

    def in_half(bb):
        return jnp.einsum('gph,gk->ghkp', bb, eye).reshape(gh * S5_GROUP_DIM, gh * S5_STATE)

    def out_half(cc):
        return jnp.einsum('ghp,gk->gpkh', cc, eye).reshape(gh * S5_STATE, gh * S5_GROUP_DIM)

    lb_re = ab_re[..., None] * bb_re - ab_im[..., None] * bb_im
    lb_im = ab_re[..., None] * bb_im + ab_im[..., None] * bb_re
    halves = [slice(i * gh, (i + 1) * gh) for i in range(S5_PARTS)]
    wcat = jnp.stack([jnp.concatenate(
        [jnp.concatenate([in_half(bb_re[s]), in_half(bb_im[s])], axis=1),
         jnp.concatenate([in_half(lb_re[s]), in_half(lb_im[s])], axis=1)], axis=0)
        for s in halves])
    cmat = jnp.stack([jnp.concatenate([out_half(c_re[s]), -out_half(c_im[s])], axis=0)
                      for s in halves])

    l2_re = (mag * mag * jnp.cos(2.0 * ang)).reshape(S5_PARTS, gh * S5_STATE)
    l2_im = (mag * mag * jnp.sin(2.0 * ang)).reshape(S5_PARTS, gh * S5_STATE)
    lam2 = jnp.concatenate([part for i in range(S5_PARTS) for part in (l2_re[i], l2_im[i])])[None, :]
    lam2 = jnp.broadcast_to(lam2, (SCAN_ROWS, 2 * S5_GROUPS * S5_STATE))
    return wcat.astype(BF16), cmat.astype(BF16), lam2


def _out_kernel(x_ref, ya_ref, ga_ref, ys_ref, gb_ref, m_ref, wglu_ref, bglu_ref, pa_ref, pb_ref,
                wout_ref, fg_ref, o_ref):
    y_a = ya_ref[...] * jax.nn.silu(ga_ref[...].astype(F32))
    proj_a = jnp.dot(y_a.astype(BF16), pa_ref[...], preferred_element_type=F32)
    ys = jax.nn.gelu(ys_ref[...])
    glu = jnp.dot(ys.astype(BF16), wglu_ref[...], preferred_element_type=F32) + bglu_ref[...]
    y_b = glu[:, :S5_WIDTH] * jax.nn.sigmoid(glu[:, S5_WIDTH:]) * jax.nn.silu(gb_ref[...].astype(F32))
    proj_b = jnp.dot(y_b.astype(BF16), pb_ref[...], preferred_element_type=F32)
    m = m_ref[...].astype(F32)
    merged = jax.nn.sigmoid(m[:, :D_MODEL]) * proj_a + jax.nn.sigmoid(m[:, D_MODEL:]) * proj_b
    x_new = x_ref[...] + jnp.dot(merged.astype(BF16), wout_ref[...], preferred_element_type=F32)
    o_ref[...] = _rms_norm(x_new, fg_ref[...])


def _out(x2, ya, ga, ys_lb, gb, m, w_glu, b_glu, p_a, p_b, w_out, final_g, tm):
    t = x2.shape[0]
    const = lambda i: (0, 0)
    tile = lambda w: pl.BlockSpec((tm, w), lambda i: (i, 0))
    full = lambda a: pl.BlockSpec(a.shape, const)
    return pl.pallas_call(
        _out_kernel,
        grid=(t // tm,),
        in_specs=[tile(D_MODEL), tile(RWKV_WIDTH), tile(RWKV_WIDTH),
                  _frame_batch_spec(tm, ys_lb.shape[0] // tm), tile(S5_WIDTH),
                  tile(2 * D_MODEL), full(w_glu), full(b_glu), full(p_a), full(p_b), full(w_out),
                  full(final_g)],
        out_specs=tile(D_MODEL),
        out_shape=jax.ShapeDtypeStruct((t, D_MODEL), F32),
        compiler_params=pltpu.CompilerParams(dimension_semantics=("arbitrary",),
                                             vmem_limit_bytes=VMEM_LIMIT_BYTES),
        name="merge_out",
    )(x2, ya, ga, ys_lb, gb, m, w_glu, b_glu, p_a, p_b, w_out, final_g)


def kernel(x, norm_g, w_in, mu_shift, w0, w_up, a0, a_up, k_k, k_a, r_k, lnx_g, lnx_b, lam_re, lam_im,
           log_dt, b_re, b_im, c_re, c_im, d_skip, w_glu, b_glu, p_a, p_b, w_out, final_g):
    bsz, seq, _ = x.shape
    assert norm_g.shape[0] == 1, "single-layer block: the final RMSNorm is fused into the output kernel"
    tm = min(512, seq)
    tt_wkv = min(256, seq)
    assert bsz == S5_BATCH, "the S5 kernel packs 2 frames x 4 batch rows per sublane block"
    rows_s5 = min(1024, seq * bsz)
    x2 = x.reshape(bsz * seq, D_MODEL)
    row = lambda p: p.reshape(1, -1).astype(F32)
    l = 0
    (rt, at, bk, v_sw, gc, bonus_sw, ga, u_lb, gb, m) = _prep(
        x2, row(norm_g[l]), w_in[l].astype(BF16), row(mu_shift[l]), row(w0[l]), w_up[l], row(a0[l]), a_up[l],
        row(k_k[l]), row(k_a[l]), row(r_k[l]), tt_wkv, bsz, seq)
    swap_row = lambda p: row(p).reshape(HEADS // 2, 2, HEAD_DIM)[:, ::-1].reshape(1, RWKV_WIDTH)
    ya = _wkv(rt, at, bk, v_sw, gc, bonus_sw, swap_row(lnx_g[l]), swap_row(lnx_b[l]), bsz, seq,
              min(2 * tt_wkv, seq))
    wcat, cmat, lam2 = _s5_params(lam_re[l], lam_im[l], log_dt[l], b_re[l], b_im[l], c_re[l], c_im[l])
    ys_lb = _s5(u_lb, wcat, cmat, lam2, row(d_skip[l]), rows_s5)
    out = _out(x2, ya, ga, ys_lb, gb, m, w_glu[l].astype(BF16), row(b_glu[l]), p_a[l].astype(BF16),
               p_b[l].astype(BF16), w_out[l].astype(BF16), row(final_g), tm)
    return out.reshape(bsz, seq, D_MODEL)
```

```python
import functools
import math

import jax
import jax.numpy as jnp
from jax import lax
from jax.experimental import pallas as pl
from jax.experimental.pallas import tpu as pltpu

F32 = jnp.float32
BF16 = jnp.bfloat16

D_MODEL = 1024
HEADS = 16
HEAD_DIM = 64
RWKV_WIDTH = HEADS * HEAD_DIM
LORA = 64
S5_GROUPS = 32
S5_GROUP_DIM = 16
S5_WIDTH = S5_GROUPS * S5_GROUP_DIM
S5_STATE = 64
SHIFT_WIDTH = 3 * RWKV_WIDTH + 2 * LORA
IN_WIDTH = SHIFT_WIDTH + RWKV_WIDTH + 2 * S5_WIDTH + 2 * D_MODEL
RMS_EPS = 1e-6
LNX_EPS = 64e-5

LANES = 128
SUBLANES = 8
CHUNK = 64
PAIR = 2 * HEAD_DIM
CHUNK_GROUP = 2
PROJ_SLICE = 512
SEG_TILE = 256
OUT_PARTS = 2
OUT_STAGES = 7
S5_PARTS = 4
S5_HALF = S5_WIDTH // S5_PARTS
S5_HALF_STATE = (S5_GROUPS // S5_PARTS) * S5_STATE
SCAN_ROWS = SUBLANES
S5_BATCH = 4

VMEM_LIMIT_BYTES = 56 * 1024 * 1024


def _split_bf16(x):
    hi = x.astype(BF16)
    lo = (x - hi.astype(F32)).astype(BF16)
    return hi, lo


def _bf16_dot(a, b):
    return jnp.dot(a.astype(BF16), b.astype(BF16), preferred_element_type=F32)


def _iota(shape, dim):
    return lax.broadcasted_iota(jnp.int32, shape, dim)


def _seg_sum(x, ones_bd):
    xb = x.astype(BF16)
    cols = [jnp.dot(xb[:, j * SEG_TILE:(j + 1) * SEG_TILE], ones_bd, preferred_element_type=F32)
            for j in range(x.shape[1] // SEG_TILE)]
    return jnp.concatenate(cols, axis=1)


def _swap_halves(x):
    n = x.shape[1]
    low = (_iota(x.shape, 1) % PAIR) < HEAD_DIM
    return jnp.where(low, pltpu.roll(x, n - HEAD_DIM, 1), pltpu.roll(x, HEAD_DIM, 1))


def _rms_norm(x, g):
    return x * lax.rsqrt(jnp.mean(x * x, axis=-1, keepdims=True) + RMS_EPS) * g


_IN_SPLITS = (0, SHIFT_WIDTH, SHIFT_WIDTH + RWKV_WIDTH, SHIFT_WIDTH + RWKV_WIDTH + S5_WIDTH,
              SHIFT_WIDTH + RWKV_WIDTH + 2 * S5_WIDTH, IN_WIDTH)


def _seg_matrix(swapped):
    seg_i = _iota((SEG_TILE, SEG_TILE), 0) // HEAD_DIM
    seg_j = _iota((SEG_TILE, SEG_TILE), 1) // HEAD_DIM
    if swapped:
        seg_j = seg_j + 1 - 2 * (seg_j % 2)
    return jnp.where(seg_i == seg_j, 1.0, 0.0).astype(BF16)


def _prep_kernel(x_ref, g_ref, w_ref, mu_ref, w0_ref, wup_ref, a0_ref, aup_ref, kk_ref, ka_ref, rk_ref,
                 rt_ref, at_ref, bk_ref, v_ref, gc_ref, bonus_ref, ga_ref, u_ref, gb_ref, m_ref,
                 prev_ref, *, tm, n_t):
    n_chunks = tm // CHUNK

    @pl.when(pl.program_id(0) % n_t == 0)
    def _():
        prev_ref[...] = jnp.zeros_like(prev_ref)

    h = _rms_norm(x_ref[...], g_ref[...]).astype(BF16)
    proj = lambda lo, hi: jnp.dot(h, w_ref[:, lo:hi], preferred_element_type=F32)

    out_cols = []
    for o_ref, lo, hi in zip((ga_ref, u_ref, gb_ref, m_ref), _IN_SPLITS[1:-1], _IN_SPLITS[2:]):
        for c0 in range(lo, hi, PROJ_SLICE):
            out_cols.append((o_ref, c0 - lo, c0, min(c0 + PROJ_SLICE, hi)))
    out_cols = iter(out_cols)

    def project(n=1):
        for _ in range(n):
            item = next(out_cols, None)
            if item is not None:
                o_ref, at_col, lo, hi = item
                o_ref[:, at_col:at_col + hi - lo] = proj(lo, hi).astype(o_ref.dtype)

    row0 = _iota((tm, 1), 0) == 0

    def shifted(z, lo, hi):
        z_prev = jnp.where(row0, prev_ref[:, lo:hi], pltpu.roll(z, 1, 0))
        return z + mu_ref[:, lo:hi] * (z_prev - z), z[tm - 1:tm, :]

    x_lo = 3 * RWKV_WIDTH
    zx, last_x = shifted(proj(x_lo, SHIFT_WIDTH), x_lo, SHIFT_WIDTH)
    zk_raw = proj(RWKV_WIDTH, 2 * RWKV_WIDTH)
    wl = w0_ref[...] + _bf16_dot(jnp.tanh(zx[:, :LORA]), wup_ref[...])
    logw = -math.exp(-0.5) * jax.nn.sigmoid(wl)
    iclr = jax.nn.sigmoid(a0_ref[...] + _bf16_dot(zx[:, LORA:], aup_ref[...]))
    zr_raw = proj(0, RWKV_WIDTH)
    ti = _iota((tm, tm), 0)
    si = _iota((tm, tm), 1)
    tril_bd = jnp.where(((ti // CHUNK) == (si // CHUNK)) & (si <= ti), 1.0, 0.0).astype(BF16)
    lw_hi, lw_lo = _split_bf16(logw)
    cum = (jnp.dot(tril_bd, lw_hi, preferred_element_type=F32)
           + jnp.dot(tril_bd, lw_lo, preferred_element_type=F32))
    k, last_k = shifted(zk_raw, RWKV_WIDTH, 2 * RWKV_WIDTH)
    zv_raw = proj(2 * RWKV_WIDTH, 3 * RWKV_WIDTH)
    kk = k * kk_ref[...]
    kk = kk * lax.rsqrt(jnp.maximum(_seg_sum(kk * kk, _seg_matrix(False)), 1e-24))
    k_h = k * (1.0 + (iclr - 1.0) * ka_ref[...])
    b = kk * iclr
    project(2)
    by_chunk = lambda t: t.reshape(n_chunks, CHUNK, RWKV_WIDTH)
    g_c = jnp.exp(by_chunk(cum)[:, CHUNK - 1:CHUNK, :])
    gc_ref[...] = jnp.broadcast_to(g_c, (n_chunks, SUBLANES, RWKV_WIDTH)).reshape(n_chunks * SUBLANES, RWKV_WIDTH)
    r, last_r = shifted(zr_raw, 0, RWKV_WIDTH)
    rt_ref[...] = r * jnp.exp(cum)
    project(2)
    at_ref[...] = -kk * jnp.exp(cum - logw)
    bonus_ref[...] = _seg_sum(r * k_h * rk_ref[...], _seg_matrix(True))
    project(2)
    v, last_v = shifted(zv_raw, 2 * RWKV_WIDTH, 3 * RWKV_WIDTH)
    v_ref[...] = _swap_halves(v)
    prev_ref[...] = jnp.concatenate([last_r, last_k, last_v, last_x], axis=1)
    project(2)
    g_inv = jnp.exp(-cum)
    b_t = b * g_inv
    k_t = k_h * g_inv
    project(2)
    for c in range(n_chunks):
        for p in range(HEADS // 2):
            blk = (slice(c * CHUNK, (c + 1) * CHUNK), slice(p * PAIR, (p + 1) * PAIR))
            dst = slice((c * (HEADS // 2) + p) * PAIR, (c * (HEADS // 2) + p + 1) * PAIR)
            bk_ref[dst, :] = jnp.concatenate([b_t[blk], k_t[blk]], axis=0).T.astype(BF16)
        project(1)
    project(64)


def _frame_batch_spec(tm, n_t):
    return pl.BlockSpec((tm, S5_WIDTH), lambda i: (i % n_t, i // n_t))


def _prep(x2, norm_g, w_in_bf16, mu, w0, w_up, a0, a_up, k_k, k_a, r_k, tm, bsz, seq):
    t = x2.shape[0]
    n_t = seq // tm
    const = lambda i: (0, 0)
    rowspec = lambda w: pl.BlockSpec((1, w), const)
    tile = lambda rows, w: pl.BlockSpec((rows, w), lambda i: (i, 0))
    pair_rows = tm // CHUNK * (HEADS // 2) * PAIR
    wide = jax.ShapeDtypeStruct((t, RWKV_WIDTH), F32)
    key_major = jax.ShapeDtypeStruct((t // CHUNK * (HEADS // 2) * PAIR, PAIR), BF16)
    out_shape = [wide, wide, key_major, wide,
                 jax.ShapeDtypeStruct((t // CHUNK * SUBLANES, RWKV_WIDTH), F32), wide,
                 jax.ShapeDtypeStruct((t, RWKV_WIDTH), BF16),
                 jax.ShapeDtypeStruct((seq, bsz * S5_WIDTH), F32),
                 jax.ShapeDtypeStruct((t, S5_WIDTH), BF16),
                 jax.ShapeDtypeStruct((t, 2 * D_MODEL), BF16)]
    out_specs = [tile(tm, RWKV_WIDTH), tile(tm, RWKV_WIDTH), tile(pair_rows, PAIR),
                 tile(tm, RWKV_WIDTH), tile(tm // CHUNK * SUBLANES, RWKV_WIDTH), tile(tm, RWKV_WIDTH),
                 tile(tm, RWKV_WIDTH), _frame_batch_spec(tm, n_t), tile(tm, S5_WIDTH), tile(tm, 2 * D_MODEL)]
    return pl.pallas_call(
        functools.partial(_prep_kernel, tm=tm, n_t=n_t),
        grid=(t // tm,),
        in_specs=[pl.BlockSpec((tm, D_MODEL), lambda i: (i, 0)),
                  rowspec(D_MODEL),
                  pl.BlockSpec((D_MODEL, IN_WIDTH), const, pipeline_mode=pl.Buffered(1)),
                  rowspec(SHIFT_WIDTH), rowspec(RWKV_WIDTH),
                  pl.BlockSpec((LORA, RWKV_WIDTH), const),
                  rowspec(RWKV_WIDTH),
                  pl.BlockSpec((LORA, RWKV_WIDTH), const),
                  rowspec(RWKV_WIDTH), rowspec(RWKV_WIDTH), rowspec(RWKV_WIDTH)],
        out_specs=out_specs,
        out_shape=out_shape,
        scratch_shapes=[pltpu.VMEM((1, SHIFT_WIDTH), F32)],
        compiler_params=pltpu.CompilerParams(dimension_semantics=("arbitrary",),
                                             vmem_limit_bytes=VMEM_LIMIT_BYTES),
        name="inproj_prep",
    )(x2, norm_g, w_in_bf16, mu, w0, w_up, a0, a_up, k_k, k_a, r_k)


def _wkv_kernel(rt_ref, at_ref, bk_ref, v_ref, gc_ref, bonus_ref, lng_ref, lnb_ref, y_ref,
                h_ref, yc_ref, *, tt):
    n_chunks = tt // CHUNK

    @pl.when(pl.program_id(1) == 0)
    def _():
        h_ref[...] = jnp.zeros_like(h_ref)

    gi = _iota((2 * CHUNK, 2 * CHUNK), 0)
    gj = _iota((2 * CHUNK, 2 * CHUNK), 1) % CHUNK
    g_mask = gj <= jnp.where(gi < CHUNK, gi, gi - (CHUNK + 1))
    lane = _iota((CHUNK, PAIR), 1)
    low = lane < HEAD_DIM
    own = (low, jnp.logical_not(low))
    zeros_cp = jnp.zeros((CHUNK, PAIR), F32)
    cast = lambda t: t.astype(BF16)
    n_pairs = HEADS // 2

    def block(c, size):
        return pl.ds(c * size, size) if isinstance(c, int) else pl.ds(pl.multiple_of(c * size, size), size)

    def phase_a(c0):
        chunks = tuple(range(CHUNK_GROUP))
        cps = [(c, p) for c in chunks for p in range(n_pairs)]
        chs = [(c, h) for c in chunks for h in range(HEADS)]
        ld = lambda ref, c, p: ref[block(c0 + c, CHUNK), p * PAIR:(p + 1) * PAIR]
        rt = {(c, p): ld(rt_ref, c, p) for c, p in cps}
        at = {(c, p): ld(at_ref, c, p) for c, p in cps}
        vs = {(c, p): ld(v_ref, c, p) for c, p in cps}

        key_major = lambda ref, c, p: ref[block((c0 + c) * n_pairs + p, PAIR), :]
        bk = {(c, p): key_major(bk_ref, c, p) for c, p in cps}

        g, x, vo, m = {}, {}, {}, {}
        for c, h in chs:
            p, s = divmod(h, 2)
            lhs = jnp.concatenate([jnp.where(own[s], rt[c, p], 0.0), jnp.where(own[s], at[c, p], 0.0)], axis=0)
            g[c, h] = cast(jnp.where(g_mask, _bf16_dot(cast(lhs), bk[c, p]), 0.0))
        for c, h in chs:
            p, s = divmod(h, 2)
            vo[c, h] = cast(jnp.where(own[s], 0.0, vs[c, p]))
            vak = _bf16_dot(g[c, h][CHUNK:, :], jnp.concatenate([cast(zeros_cp), vo[c, h]], axis=0))
            x[c, h] = jnp.where(own[s], at[c, p], 0.0) + vak
            m[c, h] = g[c, h][CHUNK:, :CHUNK]
        for step in range(5):
            for ch in chs:
                zz = _bf16_dot(m[ch], jnp.concatenate([cast(x[ch]), m[ch]], axis=1))
                x[ch] = x[ch] + zz[:, :PAIR]
                m[ch] = cast(zz[:, PAIR:])
        for ch in chs:
            x[ch] = x[ch] + _bf16_dot(m[ch], cast(x[ch]))
        o9 = {}
        for c, h in chs:
            p, s = divmod(h, 2)
            lhs = jnp.concatenate([g[c, h][:CHUNK, :], bk[c, p][s * CHUNK:(s + 1) * CHUNK, :]], axis=0)
            o9[c, h] = _bf16_dot(lhs, jnp.concatenate([cast(x[c, h]), vo[c, h]], axis=0))
        out = {}
        for c, p in cps:
            t0, t1 = o9[c, 2 * p][:CHUNK], o9[c, 2 * p + 1][:CHUNK]
            b0, b1 = o9[c, 2 * p][CHUNK:], o9[c, 2 * p + 1][CHUNK:]
            r_p = rt[c, p] + jnp.where(low, t0, t1)
            y0 = jnp.where(low, t1, t0)
            phi = jnp.concatenate([jnp.where(low, b0, 0.0), jnp.where(low, 0.0, b1)], axis=0)
            hk = jnp.concatenate([jnp.where(low, 0.0, b0), jnp.where(low, b1, 0.0)], axis=0)
            out[c, p] = (cast(jnp.concatenate([r_p, phi], axis=0)), y0, hk)
        return out

    def phase_b(c0, c, prepared, states):
        new_states = []
        for p in range(n_pairs):
            lhs, y0, hk = prepared[c, p]
            gc = gc_ref[block(c0 + c, SUBLANES), p * PAIR:(p + 1) * PAIR][0:1, :]
            g_col = jnp.broadcast_to(gc, (PAIR, PAIR)).T
            zz = _bf16_dot(lhs, states[p])
            yc_ref[block(c0 + c, CHUNK), p * PAIR:(p + 1) * PAIR] = zz[:CHUNK] + y0
            new_states.append(g_col * (states[p] + (zz[CHUNK:] + hk)))
        return new_states

    states = [h_ref[p] for p in range(n_pairs)]
    for c0 in range(0, n_chunks, CHUNK_GROUP):
        prepared = phase_a(c0)
        for c in range(CHUNK_GROUP):
            states = phase_b(c0, c, prepared, states)
    for p in range(n_pairs):
        h_ref[p] = states[p]

    ones_bd = _seg_matrix(False)
    y = yc_ref[...]
    mean = _seg_sum(y, ones_bd) * (1.0 / HEAD_DIM)
    yc = y - mean
    var = _seg_sum(yc * yc, ones_bd) * (1.0 / HEAD_DIM)
    yn = yc * lax.rsqrt(var + LNX_EPS) * lng_ref[...] + lnb_ref[...]
    y_ref[...] = _swap_halves(yn + bonus_ref[...] * v_ref[...])


def _wkv(rt, at, bk, v_sw, gc, bonus_sw, lnx_g_sw, lnx_b_sw, bsz, seq, tt):
    n_t = seq // tt
    const = lambda b, i: (0, 0)
    tile = lambda rows, w: pl.BlockSpec((rows, w), lambda b, i: (b * n_t + i, 0))
    pair_rows = tt // CHUNK * (HEADS // 2) * PAIR
    return pl.pallas_call(
        functools.partial(_wkv_kernel, tt=tt),
        grid=(bsz, n_t),
        in_specs=[tile(tt, RWKV_WIDTH), tile(tt, RWKV_WIDTH), tile(pair_rows, PAIR),
                  tile(tt, RWKV_WIDTH), tile(tt // CHUNK * SUBLANES, RWKV_WIDTH), tile(tt, RWKV_WIDTH),
                  pl.BlockSpec((1, RWKV_WIDTH), const), pl.BlockSpec((1, RWKV_WIDTH), const)],
        out_specs=tile(tt, RWKV_WIDTH),
        out_shape=jax.ShapeDtypeStruct((bsz * seq, RWKV_WIDTH), F32),
        scratch_shapes=[pltpu.VMEM((HEADS // 2, PAIR, PAIR), F32),
                        pltpu.VMEM((tt, RWKV_WIDTH), F32)],
        compiler_params=pltpu.CompilerParams(dimension_semantics=("arbitrary", "arbitrary"),
                                             vmem_limit_bytes=VMEM_LIMIT_BYTES),
        name="wkv7",
    )(rt, at, bk, v_sw, gc, bonus_sw, lnx_g_sw, lnx_b_sw)


def _s5_kernel(u_ref, w_ref, cm_ref, lam2_ref, d_ref, y_ref, ut_ref, prev_ref, carry_ref, st_ref, *, rows):
    @pl.when(pl.program_id(0) == 0)
    def _():
        prev_ref[...] = jnp.zeros_like(prev_ref)
        carry_ref[...] = jnp.zeros_like(carry_ref)

    frames = rows // S5_BATCH
    n_col = S5_WIDTH // LANES
    for b in range(S5_BATCH):
        for j in range(n_col):
            col = b * S5_WIDTH + j * LANES
            ut_ref[j, pl.ds(b, frames, stride=S5_BATCH), :] = u_ref[:, col:col + LANES]
    u = jnp.concatenate([ut_ref[j] for j in range(n_col)], axis=1)
    hs = S5_HALF_STATE
    shifted = pltpu.roll(u, S5_BATCH, 0)
    row8 = _iota((SCAN_ROWS, S5_WIDTH), 0)
    head = jnp.where(row8 < S5_BATCH, pltpu.roll(prev_ref[...], S5_BATCH, 0), shifted[:SCAN_ROWS])
    u_prev = jnp.concatenate([head, shifted[SCAN_ROWS:]], axis=0)
    prev_ref[...] = u[rows - SCAN_ROWS:, :]
    ub = u.astype(BF16)
    upb = u_prev.astype(BF16)
    for half in range(S5_PARTS):
        cols = slice(half * S5_HALF, (half + 1) * S5_HALF)
        st_ref[:, half * 2 * hs:(half + 1) * 2 * hs] = jnp.dot(
            jnp.concatenate([ub[:, cols], upb[:, cols]], axis=1), w_ref[half], preferred_element_type=F32)

    n_blocks = rows // SCAN_ROWS
    for half in range(S5_PARTS):
        re_l = pl.ds(half * 2 * hs, hs)
        im_l = pl.ds(half * 2 * hs + hs, hs)
        l_re = lam2_ref[:, re_l]
        l_im = lam2_ref[:, im_l]

        def block(i, carry, re_l=re_l, im_l=im_l, l_re=l_re, l_im=l_im):
            c_re, c_im = carry
            blk = pl.ds(i * SCAN_ROWS, SCAN_ROWS)
            x_re = st_ref[blk, re_l] + (l_re * c_re - l_im * c_im)
            x_im = st_ref[blk, im_l] + (l_re * c_im + l_im * c_re)
            st_ref[blk, re_l] = x_re
            st_ref[blk, im_l] = x_im
            return x_re, x_im

        carry = (carry_ref[:, re_l], carry_ref[:, im_l])
        for i in range(n_blocks):
            carry = block(i, carry)
        c_re, c_im = carry
        carry_ref[:, re_l] = c_re
        carry_ref[:, im_l] = c_im

    outs = [jnp.dot(st_ref[:, half * 2 * hs:(half + 1) * 2 * hs].astype(BF16), cm_ref[half],
                    preferred_element_type=F32) for half in range(S5_PARTS)]
    y = jnp.concatenate(outs, axis=1) + d_ref[...] * u
    for j in range(n_col):
        ut_ref[j] = y[:, j * LANES:(j + 1) * LANES]
    for b in range(S5_BATCH):
        for j in range(n_col):
            col = b * S5_WIDTH + j * LANES
            y_ref[:, col:col + LANES] = ut_ref[j, pl.ds(b, frames, stride=S5_BATCH), :]


def _s5(u_lb, wcat, cmat, lam2, d_row, rows):
    seq = u_lb.shape[0]
    frames = rows // S5_BATCH
    hs = S5_HALF_STATE
    const2 = lambda i: (0, 0)
    const3 = lambda i: (0, 0, 0)
    return pl.pallas_call(
        functools.partial(_s5_kernel, rows=rows),
        grid=(seq // frames,),
        in_specs=[pl.BlockSpec((frames, S5_BATCH * S5_WIDTH), lambda i: (i, 0)),
                  pl.BlockSpec((S5_PARTS, 2 * S5_HALF, 2 * hs), const3),
                  pl.BlockSpec((S5_PARTS, 2 * hs, S5_HALF), const3),
                  pl.BlockSpec((SCAN_ROWS, 2 * S5_PARTS * hs), const2),
                  pl.BlockSpec((1, S5_WIDTH), const2)],
        out_specs=pl.BlockSpec((frames, S5_BATCH * S5_WIDTH), lambda i: (i, 0)),
        out_shape=jax.ShapeDtypeStruct((seq, S5_BATCH * S5_WIDTH), F32),
        scratch_shapes=[pltpu.VMEM((S5_WIDTH // LANES, rows, LANES), F32),
                        pltpu.VMEM((SCAN_ROWS, S5_WIDTH), F32),
                        pltpu.VMEM((SCAN_ROWS, 2 * S5_PARTS * hs), F32),
                        pltpu.VMEM((rows, 2 * S5_PARTS * hs), F32)],
        compiler_params=pltpu.CompilerParams(dimension_semantics=("arbitrary",),
                                             vmem_limit_bytes=VMEM_LIMIT_BYTES),
        name="s5",
    )(u_lb, wcat, cmat, lam2, d_row)


def _s5_params(lam_re, lam_im, log_dt, b_re, b_im, c_re, c_im):
    dt = jnp.exp(log_dt)[:, None]
    mag = jnp.exp(lam_re * dt)
    ang = lam_im * dt
    ab_re = mag * jnp.cos(ang)
    ab_im = mag * jnp.sin(ang)
    den = lam_re * lam_re + lam_im * lam_im
    nr = ab_re - 1.0
    f_re = (nr * lam_re + ab_im * lam_im) / den
    f_im = (ab_im * lam_re - nr * lam_im) / den
    bb_re = f_re[..., None] * b_re - f_im[..., None] * b_im
    bb_im = f_re[..., None] * b_im + f_im[..., None] * b_re

    gh = S5_GROUPS // S5_PARTS
    eye = jnp.eye(gh, dtype=F32)

    def in_half(bb):
        return jnp.einsum('gph,gk->ghkp', bb, eye).reshape(gh * S5_GROUP_DIM, gh * S5_STATE)

    def out_half(cc):
        return jnp.einsum('ghp,gk->gpkh', cc, eye).reshape(gh * S5_STATE, gh * S5_GROUP_DIM)

    lb_re = ab_re[..., None] * bb_re - ab_im[..., None] * bb_im
    lb_im = ab_re[..., None] * bb_im + ab_im[..., None] * bb_re
    halves = [slice(i * gh, (i + 1) * gh) for i in range(S5_PARTS)]
    wcat = jnp.stack([jnp.concatenate(
        [jnp.concatenate([in_half(bb_re[s]), in_half(bb_im[s])], axis=1),
         jnp.concatenate([in_half(lb_re[s]), in_half(lb_im[s])], axis=1)], axis=0)
        for s in halves])
    cmat = jnp.stack([jnp.concatenate([out_half(c_re[s]), -out_half(c_im[s])], axis=0)
                      for s in halves])

    l2_re = (mag * mag * jnp.cos(2.0 * ang)).reshape(S5_PARTS, gh * S5_STATE)
    l2_im = (mag * mag * jnp.sin(2.0 * ang)).reshape(S5_PARTS, gh * S5_STATE)
    lam2 = jnp.concatenate([part for i in range(S5_PARTS) for part in (l2_re[i], l2_im[i])])[None, :]
    lam2 = jnp.broadcast_to(lam2, (SCAN_ROWS, 2 * S5_GROUPS * S5_STATE))
    return wcat.astype(BF16), cmat.astype(BF16), lam2


def _out_kernel(x_ref, ya_ref, ga_ref, ys_ref, gb_ref, m_ref, wglu_ref, bglu_ref, pa_ref, pb_ref,
                wout_ref, fg_ref, o_ref):
    rows_per = x_ref.shape[0] // OUT_PARTS

    def part(i):
        rows = slice(i * rows_per, (i + 1) * rows_per)
        y_a = ya_ref[rows, :] * jax.nn.silu(ga_ref[rows, :].astype(F32))
        ys = jax.nn.gelu(ys_ref[rows, :])
        yield
        proj_a = jnp.dot(y_a.astype(BF16), pa_ref[...], preferred_element_type=F32)
        glu = jnp.dot(ys.astype(BF16), wglu_ref[...], preferred_element_type=F32) + bglu_ref[...]
        yield
        y_b = glu[:, :S5_WIDTH] * jax.nn.sigmoid(glu[:, S5_WIDTH:]) * jax.nn.silu(gb_ref[rows, :].astype(F32))
        yield
        proj_b = jnp.dot(y_b.astype(BF16), pb_ref[...], preferred_element_type=F32)
        yield
        m = m_ref[rows, :].astype(F32)
        merged = jax.nn.sigmoid(m[:, :D_MODEL]) * proj_a + jax.nn.sigmoid(m[:, D_MODEL:]) * proj_b
        yield
        x_new = x_ref[rows, :] + jnp.dot(merged.astype(BF16), wout_ref[...], preferred_element_type=F32)
        yield
        o_ref[rows, :] = _rms_norm(x_new, fg_ref[...])
        yield

    parts = [part(i) for i in range(OUT_PARTS)]
    for t in range(OUT_STAGES + OUT_PARTS - 1):
        for i, gen in enumerate(parts):
            if 0 <= t - i < OUT_STAGES:
                next(gen)


def _out(x2, ya, ga, ys_lb, gb, m, w_glu, b_glu, p_a, p_b, w_out, final_g, tm):
    t = x2.shape[0]
    const = lambda i: (0, 0)
    tile = lambda w: pl.BlockSpec((tm, w), lambda i: (i, 0))
    full = lambda a: pl.BlockSpec(a.shape, const)
    return pl.pallas_call(
        _out_kernel,
        grid=(t // tm,),
        in_specs=[tile(D_MODEL), tile(RWKV_WIDTH), tile(RWKV_WIDTH),
                  _frame_batch_spec(tm, ys_lb.shape[0] // tm), tile(S5_WIDTH),
                  tile(2 * D_MODEL), full(w_glu), full(b_glu), full(p_a), full(p_b), full(w_out),
                  full(final_g)],
        out_specs=tile(D_MODEL),
        out_shape=jax.ShapeDtypeStruct((t, D_MODEL), F32),
        compiler_params=pltpu.CompilerParams(dimension_semantics=("arbitrary",),
                                             vmem_limit_bytes=VMEM_LIMIT_BYTES),
        name="merge_out",
    )(x2, ya, ga, ys_lb, gb, m, w_glu, b_glu, p_a, p_b, w_out, final_g)


def kernel(x, norm_g, w_in, mu_shift, w0, w_up, a0, a_up, k_k, k_a, r_k, lnx_g, lnx_b, lam_re, lam_im,
           log_dt, b_re, b_im, c_re, c_im, d_skip, w_glu, b_glu, p_a, p_b, w_out, final_g):
    bsz, seq, _ = x.shape
    assert norm_g.shape[0] == 1, "single-layer block: the final RMSNorm is fused into the output kernel"
    tm = min(512, seq)
    tt_wkv = min(256, seq)
    assert bsz == S5_BATCH, "the S5 kernel packs 2 frames x 4 batch rows per sublane block"
    rows_s5 = min(1024, seq * bsz)
    x2 = x.reshape(bsz * seq, D_MODEL)
    row = lambda p: p.reshape(1, -1).astype(F32)
    l = 0
    (rt, at, bk, v_sw, gc, bonus_sw, ga, u_lb, gb, m) = _prep(
        x2, row(norm_g[l]), w_in[l].astype(BF16), row(mu_shift[l]), row(w0[l]), w_up[l], row(a0[l]), a_up[l],
        row(k_k[l]), row(k_a[l]), row(r_k[l]), tt_wkv, bsz, seq)
    swap_row = lambda p: row(p).reshape(HEADS // 2, 2, HEAD_DIM)[:, ::-1].reshape(1, RWKV_WIDTH)
    ya = _wkv(rt, at, bk, v_sw, gc, bonus_sw, swap_row(lnx_g[l]), swap_row(lnx_b[l]), bsz, seq,
              min(2 * tt_wkv, seq))
    wcat, cmat, lam2 = _s5_params(lam_re[l], lam_im[l], log_dt[l], b_re[l], b_im[l], c_re[l], c_im[l])
    ys_lb = _s5(u_lb, wcat, cmat, lam2, row(d_skip[l]), rows_s5)
    out = _out(x2, ya, ga, ys_lb, gb, m, w_glu[l].astype(BF16), row(b_glu[l]), p_a[l].astype(BF16),
               p_b[l].astype(BF16), w_out[l].astype(BF16), row(final_g), tm)
    return out.reshape(bsz, seq, D_MODEL)
```

```python
import functools
import math

import jax
import jax.numpy as jnp
from jax import lax
from jax.experimental import pallas as pl
from jax.experimental.pallas import tpu as pltpu

F32 = jnp.float32
BF16 = jnp.bfloat16

D_MODEL = 1024
HEADS = 16
HEAD_DIM = 64
RWKV_WIDTH = HEADS * HEAD_DIM
LORA = 64
S5_GROUPS = 32
S5_GROUP_DIM = 16
S5_WIDTH = S5_GROUPS * S5_GROUP_DIM
S5_STATE = 64
SHIFT_WIDTH = 3 * RWKV_WIDTH + 2 * LORA
IN_WIDTH = SHIFT_WIDTH + RWKV_WIDTH + 2 * S5_WIDTH + 2 * D_MODEL
RMS_EPS = 1e-6
LNX_EPS = 64e-5

LANES = 128
SUBLANES = 8
CHUNK = 64
PAIR = 2 * HEAD_DIM
CHUNK_GROUP = 2
PREP_WIDTH = 512
PROJ_SLICE = 512
SEG_TILE = 256
S5_PARTS = 4
S5_HALF = S5_WIDTH // S5_PARTS
S5_HALF_STATE = (S5_GROUPS // S5_PARTS) * S5_STATE
SCAN_ROWS = SUBLANES
S5_BATCH = 4

VMEM_LIMIT_BYTES = 56 * 1024 * 1024


def _split_bf16(x):
    hi = x.astype(BF16)
    lo = (x - hi.astype(F32)).astype(BF16)
    return hi, lo


def _bf16_dot(a, b):
    return jnp.dot(a.astype(BF16), b.astype(BF16), preferred_element_type=F32)


def _iota(shape, dim):
    return lax.broadcasted_iota(jnp.int32, shape, dim)


def _seg_sum(x, ones_bd):
    xb = x.astype(BF16)
    cols = [jnp.dot(xb[:, j * SEG_TILE:(j + 1) * SEG_TILE], ones_bd, preferred_element_type=F32)
            for j in range(x.shape[1] // SEG_TILE)]
    return jnp.concatenate(cols, axis=1)


def _swap_halves(x):
    n = x.shape[1]
    low = (_iota(x.shape, 1) % PAIR) < HEAD_DIM
    return jnp.where(low, pltpu.roll(x, n - HEAD_DIM, 1), pltpu.roll(x, HEAD_DIM, 1))


def _rms_norm(x, g):
    return x * lax.rsqrt(jnp.mean(x * x, axis=-1, keepdims=True) + RMS_EPS) * g


_IN_SPLITS = (0, SHIFT_WIDTH, SHIFT_WIDTH + RWKV_WIDTH, SHIFT_WIDTH + RWKV_WIDTH + S5_WIDTH,
              SHIFT_WIDTH + RWKV_WIDTH + 2 * S5_WIDTH, IN_WIDTH)


def _seg_matrix(swapped):
    seg_i = _iota((SEG_TILE, SEG_TILE), 0) // HEAD_DIM
    seg_j = _iota((SEG_TILE, SEG_TILE), 1) // HEAD_DIM
    if swapped:
        seg_j = seg_j + 1 - 2 * (seg_j % 2)
    return jnp.where(seg_i == seg_j, 1.0, 0.0).astype(BF16)


def _prep_kernel(x_ref, g_ref, w_ref, mu_ref, w0_ref, wup_ref, a0_ref, aup_ref, kk_ref, ka_ref, rk_ref,
                 rt_ref, at_ref, bk_ref, v_ref, gc_ref, bonus_ref, ga_ref, u_ref, gb_ref, m_ref,
                 prev_ref, *, tm, n_t):
    n_chunks = tm // CHUNK

    @pl.when(pl.program_id(0) % n_t == 0)
    def _():
        prev_ref[...] = jnp.zeros_like(prev_ref)

    h = _rms_norm(x_ref[...], g_ref[...]).astype(BF16)
    proj = lambda lo, hi: jnp.dot(h, w_ref[:, lo:hi], preferred_element_type=F32)

    out_cols = []
    for o_ref, lo, hi in zip((ga_ref, u_ref, gb_ref, m_ref), _IN_SPLITS[1:-1], _IN_SPLITS[2:]):
        for c0 in range(lo, hi, PROJ_SLICE):
            out_cols.append((o_ref, c0 - lo, c0, min(c0 + PROJ_SLICE, hi)))
    out_cols = iter(out_cols)

    def project(n=1):
        for _ in range(n):
            item = next(out_cols, None)
            if item is not None:
                o_ref, at_col, lo, hi = item
                o_ref[:, at_col:at_col + hi - lo] = proj(lo, hi).astype(o_ref.dtype)

    row0 = _iota((tm, 1), 0) == 0

    def shifted(z, lo, hi):
        z_prev = jnp.where(row0, prev_ref[:, lo:hi], pltpu.roll(z, 1, 0))
        return z + mu_ref[:, lo:hi] * (z_prev - z), z[tm - 1:tm, :]

    x_lo = 3 * RWKV_WIDTH
    zx, last_x = shifted(proj(x_lo, SHIFT_WIDTH), x_lo, SHIFT_WIDTH)
    prev_ref[:, x_lo:SHIFT_WIDTH] = last_x
    tanh_xw = jnp.tanh(zx[:, :LORA])
    xa = zx[:, LORA:]
    ti = _iota((tm, tm), 0)
    si = _iota((tm, tm), 1)
    tril_bd = jnp.where(((ti // CHUNK) == (si // CHUNK)) & (si <= ti), 1.0, 0.0).astype(BF16)
    by_chunk = lambda t: t.reshape(n_chunks, CHUNK, PREP_WIDTH)

    for part in range(RWKV_WIDTH // PREP_WIDTH):
        lo = part * PREP_WIDTH
        cols = slice(lo, lo + PREP_WIDTH)
        kcols = (RWKV_WIDTH + lo, RWKV_WIDTH + lo + PREP_WIDTH)
        vcols = (2 * RWKV_WIDTH + lo, 2 * RWKV_WIDTH + lo + PREP_WIDTH)
        zk_raw = proj(*kcols)
        wl = w0_ref[:, cols] + _bf16_dot(tanh_xw, wup_ref[:, cols])
        logw = -math.exp(-0.5) * jax.nn.sigmoid(wl)
        iclr = jax.nn.sigmoid(a0_ref[:, cols] + _bf16_dot(xa, aup_ref[:, cols]))
        zr_raw = proj(lo, lo + PREP_WIDTH)
        lw_hi, lw_lo = _split_bf16(logw)
        cum = (jnp.dot(tril_bd, lw_hi, preferred_element_type=F32)
               + jnp.dot(tril_bd, lw_lo, preferred_element_type=F32))
        k, last_k = shifted(zk_raw, *kcols)
        prev_ref[:, kcols[0]:kcols[1]] = last_k
        zv_raw = proj(*vcols)
        kk = k * kk_ref[:, cols]
        kk = kk * lax.rsqrt(jnp.maximum(_seg_sum(kk * kk, _seg_matrix(False)), 1e-24))
        k_h = k * (1.0 + (iclr - 1.0) * ka_ref[:, cols])
        b = kk * iclr
        project(1)
        g_c = jnp.exp(by_chunk(cum)[:, CHUNK - 1:CHUNK, :])
        gc_ref[:, cols] = jnp.broadcast_to(g_c, (n_chunks, SUBLANES, PREP_WIDTH)).reshape(n_chunks * SUBLANES,
                                                                                         PREP_WIDTH)
        r, last_r = shifted(zr_raw, lo, lo + PREP_WIDTH)
        prev_ref[:, cols] = last_r
        rt_ref[:, cols] = r * jnp.exp(cum)
        project(1)
        at_ref[:, cols] = -kk * jnp.exp(cum - logw)
        bonus_ref[:, cols] = _seg_sum(r * k_h * rk_ref[:, cols], _seg_matrix(True))
        project(1)
        v, last_v = shifted(zv_raw, *vcols)
        prev_ref[:, vcols[0]:vcols[1]] = last_v
        v_ref[:, cols] = _swap_halves(v)
        g_inv = jnp.exp(-cum)
        b_t = b * g_inv
        k_t = k_h * g_inv
        project(1)
        pairs_per_part = PREP_WIDTH // PAIR
        for c in range(n_chunks):
            for q in range(pairs_per_part):
                p = part * pairs_per_part + q
                blk = (slice(c * CHUNK, (c + 1) * CHUNK), slice(q * PAIR, (q + 1) * PAIR))
                dst = slice((c * (HEADS // 2) + p) * PAIR, (c * (HEADS // 2) + p + 1) * PAIR)
                bk_ref[dst, :] = jnp.concatenate([b_t[blk], k_t[blk]], axis=0).T.astype(BF16)
    project(64)


def _frame_batch_spec(tm, n_t):
    return pl.BlockSpec((tm, S5_WIDTH), lambda i: (i % n_t, i // n_t))


def _prep(x2, norm_g, w_in_bf16, mu, w0, w_up, a0, a_up, k_k, k_a, r_k, tm, bsz, seq):
    t = x2.shape[0]
    n_t = seq // tm
    const = lambda i: (0, 0)
    rowspec = lambda w: pl.BlockSpec((1, w), const)
    tile = lambda rows, w: pl.BlockSpec((rows, w), lambda i: (i, 0))
    pair_rows = tm // CHUNK * (HEADS // 2) * PAIR
    wide = jax.ShapeDtypeStruct((t, RWKV_WIDTH), F32)
    key_major = jax.ShapeDtypeStruct((t // CHUNK * (HEADS // 2) * PAIR, PAIR), BF16)
    out_shape = [wide, wide, key_major, wide,
                 jax.ShapeDtypeStruct((t // CHUNK * SUBLANES, RWKV_WIDTH), F32), wide,
                 jax.ShapeDtypeStruct((t, RWKV_WIDTH), BF16),
                 jax.ShapeDtypeStruct((seq, bsz * S5_WIDTH), F32),
                 jax.ShapeDtypeStruct((t, S5_WIDTH), BF16),
                 jax.ShapeDtypeStruct((t, 2 * D_MODEL), BF16)]
    out_specs = [tile(tm, RWKV_WIDTH), tile(tm, RWKV_WIDTH), tile(pair_rows, PAIR),
                 tile(tm, RWKV_WIDTH), tile(tm // CHUNK * SUBLANES, RWKV_WIDTH), tile(tm, RWKV_WIDTH),
                 tile(tm, RWKV_WIDTH), _frame_batch_spec(tm, n_t), tile(tm, S5_WIDTH), tile(tm, 2 * D_MODEL)]
    return pl.pallas_call(
        functools.partial(_prep_kernel, tm=tm, n_t=n_t),
        grid=(t // tm,),
        in_specs=[pl.BlockSpec((tm, D_MODEL), lambda i: (i, 0)),
                  rowspec(D_MODEL),
                  pl.BlockSpec((D_MODEL, IN_WIDTH), const, pipeline_mode=pl.Buffered(1)),
                  rowspec(SHIFT_WIDTH), rowspec(RWKV_WIDTH),
                  pl.BlockSpec((LORA, RWKV_WIDTH), const),
                  rowspec(RWKV_WIDTH),
                  pl.BlockSpec((LORA, RWKV_WIDTH), const),
                  rowspec(RWKV_WIDTH), rowspec(RWKV_WIDTH), rowspec(RWKV_WIDTH)],
        out_specs=out_specs,
        out_shape=out_shape,
        scratch_shapes=[pltpu.VMEM((1, SHIFT_WIDTH), F32)],
        compiler_params=pltpu.CompilerParams(dimension_semantics=("arbitrary",),
                                             vmem_limit_bytes=VMEM_LIMIT_BYTES),
        name="inproj_prep",
    )(x2, norm_g, w_in_bf16, mu, w0, w_up, a0, a_up, k_k, k_a, r_k)


def _wkv_kernel(rt_ref, at_ref, bk_ref, v_ref, gc_ref, bonus_ref, lng_ref, lnb_ref, y_ref,
                h_ref, yc_ref, *, tt):
    n_chunks = tt // CHUNK

    @pl.when(pl.program_id(1) == 0)
    def _():
        h_ref[...] = jnp.zeros_like(h_ref)

    gi = _iota((2 * CHUNK, 2 * CHUNK), 0)
    gj = _iota((2 * CHUNK, 2 * CHUNK), 1) % CHUNK
    g_mask = gj <= jnp.where(gi < CHUNK, gi, gi - (CHUNK + 1))
    lane = _iota((CHUNK, PAIR), 1)
    low = lane < HEAD_DIM
    own = (low, jnp.logical_not(low))
    zeros_cp = jnp.zeros((CHUNK, PAIR), F32)
    cast = lambda t: t.astype(BF16)
    n_pairs = HEADS // 2

    def block(c, size):
        return pl.ds(c * size, size) if isinstance(c, int) else pl.ds(pl.multiple_of(c * size, size), size)

    def phase_a(c0):
        chunks = tuple(range(CHUNK_GROUP))
        cps = [(c, p) for c in chunks for p in range(n_pairs)]
        chs = [(c, h) for c in chunks for h in range(HEADS)]
        ld = lambda ref, c, p: ref[block(c0 + c, CHUNK), p * PAIR:(p + 1) * PAIR]
        rt = {(c, p): ld(rt_ref, c, p) for c, p in cps}
        at = {(c, p): ld(at_ref, c, p) for c, p in cps}
        vs = {(c, p): ld(v_ref, c, p) for c, p in cps}

        key_major = lambda ref, c, p: ref[block((c0 + c) * n_pairs + p, PAIR), :]
        bk = {(c, p): key_major(bk_ref, c, p) for c, p in cps}

        g, x, vo, m = {}, {}, {}, {}
        for c, h in chs:
            p, s = divmod(h, 2)
            lhs = jnp.concatenate([jnp.where(own[s], rt[c, p], 0.0), jnp.where(own[s], at[c, p], 0.0)], axis=0)
            g[c, h] = cast(jnp.where(g_mask, _bf16_dot(cast(lhs), bk[c, p]), 0.0))
        for c, h in chs:
            p, s = divmod(h, 2)
            vo[c, h] = cast(jnp.where(own[s], 0.0, vs[c, p]))
            vak = _bf16_dot(g[c, h][CHUNK:, :], jnp.concatenate([cast(zeros_cp), vo[c, h]], axis=0))
            x[c, h] = jnp.where(own[s], at[c, p], 0.0) + vak
            m[c, h] = g[c, h][CHUNK:, :CHUNK]
        for step in range(5):
            for ch in chs:
                zz = _bf16_dot(m[ch], jnp.concatenate([cast(x[ch]), m[ch]], axis=1))
                x[ch] = x[ch] + zz[:, :PAIR]
                m[ch] = cast(zz[:, PAIR:])
        for ch in chs:
            x[ch] = x[ch] + _bf16_dot(m[ch], cast(x[ch]))
        o9 = {}
        for c, h in chs:
            p, s = divmod(h, 2)
            lhs = jnp.concatenate([g[c, h][:CHUNK, :], bk[c, p][s * CHUNK:(s + 1) * CHUNK, :]], axis=0)
            o9[c, h] = _bf16_dot(lhs, jnp.concatenate([cast(x[c, h]), vo[c, h]], axis=0))
        out = {}
        for c, p in cps:
            t0, t1 = o9[c, 2 * p][:CHUNK], o9[c, 2 * p + 1][:CHUNK]
            b0, b1 = o9[c, 2 * p][CHUNK:], o9[c, 2 * p + 1][CHUNK:]
            r_p = rt[c, p] + jnp.where(low, t0, t1)
            y0 = jnp.where(low, t1, t0)
            phi = jnp.concatenate([jnp.where(low, b0, 0.0), jnp.where(low, 0.0, b1)], axis=0)
            hk = jnp.concatenate([jnp.where(low, 0.0, b0), jnp.where(low, b1, 0.0)], axis=0)
            out[c, p] = (cast(jnp.concatenate([r_p, phi], axis=0)), y0, hk)
        return out

    def phase_b(c0, c, prepared, states):
        new_states = []
        for p in range(n_pairs):
            lhs, y0, hk = prepared[c, p]
            gc = gc_ref[block(c0 + c, SUBLANES), p * PAIR:(p + 1) * PAIR][0:1, :]
            g_col = jnp.broadcast_to(gc, (PAIR, PAIR)).T
            zz = _bf16_dot(lhs, states[p])
            yc_ref[block(c0 + c, CHUNK), p * PAIR:(p + 1) * PAIR] = zz[:CHUNK] + y0
            new_states.append(g_col * (states[p] + (zz[CHUNK:] + hk)))
        return new_states

    states = [h_ref[p] for p in range(n_pairs)]
    for c0 in range(0, n_chunks, CHUNK_GROUP):
        prepared = phase_a(c0)
        for c in range(CHUNK_GROUP):
            states = phase_b(c0, c, prepared, states)
    for p in range(n_pairs):
        h_ref[p] = states[p]

    ones_bd = _seg_matrix(False)
    y = yc_ref[...]
    mean = _seg_sum(y, ones_bd) * (1.0 / HEAD_DIM)
    yc = y - mean
    var = _seg_sum(yc * yc, ones_bd) * (1.0 / HEAD_DIM)
    yn = yc * lax.rsqrt(var + LNX_EPS) * lng_ref[...] + lnb_ref[...]
    y_ref[...] = _swap_halves(yn + bonus_ref[...] * v_ref[...])


def _wkv(rt, at, bk, v_sw, gc, bonus_sw, lnx_g_sw, lnx_b_sw, bsz, seq, tt):
    n_t = seq // tt
    const = lambda b, i: (0, 0)
    tile = lambda rows, w: pl.BlockSpec((rows, w), lambda b, i: (b * n_t + i, 0))
    pair_rows = tt // CHUNK * (HEADS // 2) * PAIR
    return pl.pallas_call(
        functools.partial(_wkv_kernel, tt=tt),
        grid=(bsz, n_t),
        in_specs=[tile(tt, RWKV_WIDTH), tile(tt, RWKV_WIDTH), tile(pair_rows, PAIR),
                  tile(tt, RWKV_WIDTH), tile(tt // CHUNK * SUBLANES, RWKV_WIDTH), tile(tt, RWKV_WIDTH),
                  pl.BlockSpec((1, RWKV_WIDTH), const), pl.BlockSpec((1, RWKV_WIDTH), const)],
        out_specs=tile(tt, RWKV_WIDTH),
        out_shape=jax.ShapeDtypeStruct((bsz * seq, RWKV_WIDTH), F32),
        scratch_shapes=[pltpu.VMEM((HEADS // 2, PAIR, PAIR), F32),
                        pltpu.VMEM((tt, RWKV_WIDTH), F32)],
        compiler_params=pltpu.CompilerParams(dimension_semantics=("arbitrary", "arbitrary"),
                                             vmem_limit_bytes=VMEM_LIMIT_BYTES),
        name="wkv7",
    )(rt, at, bk, v_sw, gc, bonus_sw, lnx_g_sw, lnx_b_sw)


def _s5_kernel(u_ref, w_ref, cm_ref, lam2_ref, d_ref, y_ref, ut_ref, prev_ref, carry_ref, st_ref, *, rows):
    @pl.when(pl.program_id(0) == 0)
    def _():
        prev_ref[...] = jnp.zeros_like(prev_ref)
        carry_ref[...] = jnp.zeros_like(carry_ref)

    frames = rows // S5_BATCH
    n_col = S5_WIDTH // LANES
    for b in range(S5_BATCH):
        for j in range(n_col):
            col = b * S5_WIDTH + j * LANES
            ut_ref[j, pl.ds(b, frames, stride=S5_BATCH), :] = u_ref[:, col:col + LANES]
    u = jnp.concatenate([ut_ref[j] for j in range(n_col)], axis=1)
    hs = S5_HALF_STATE
    shifted = pltpu.roll(u, S5_BATCH, 0)
    row8 = _iota((SCAN_ROWS, S5_WIDTH), 0)
    head = jnp.where(row8 < S5_BATCH, pltpu.roll(prev_ref[...], S5_BATCH, 0), shifted[:SCAN_ROWS])
    u_prev = jnp.concatenate([head, shifted[SCAN_ROWS:]], axis=0)
    prev_ref[...] = u[rows - SCAN_ROWS:, :]
    ub = u.astype(BF16)
    upb = u_prev.astype(BF16)
    for half in range(S5_PARTS):
        cols = slice(half * S5_HALF, (half + 1) * S5_HALF)
        st_ref[:, half * 2 * hs:(half + 1) * 2 * hs] = jnp.dot(
            jnp.concatenate([ub[:, cols], upb[:, cols]], axis=1), w_ref[half], preferred_element_type=F32)

    n_blocks = rows // SCAN_ROWS
    for half in range(S5_PARTS):
        re_l = pl.ds(half * 2 * hs, hs)
        im_l = pl.ds(half * 2 * hs + hs, hs)
        l_re = lam2_ref[:, re_l]
        l_im = lam2_ref[:, im_l]

        def block(i, carry, re_l=re_l, im_l=im_l, l_re=l_re, l_im=l_im):
            c_re, c_im = carry
            blk = pl.ds(i * SCAN_ROWS, SCAN_ROWS)
            x_re = st_ref[blk, re_l] + (l_re * c_re - l_im * c_im)
            x_im = st_ref[blk, im_l] + (l_re * c_im + l_im * c_re)
            st_ref[blk, re_l] = x_re
            st_ref[blk, im_l] = x_im
            return x_re, x_im

        carry = (carry_ref[:, re_l], carry_ref[:, im_l])
        for i in range(n_blocks):
            carry = block(i, carry)
        c_re, c_im = carry
        carry_ref[:, re_l] = c_re
        carry_ref[:, im_l] = c_im

    outs = [jnp.dot(st_ref[:, half * 2 * hs:(half + 1) * 2 * hs].astype(BF16), cm_ref[half],
                    preferred_element_type=F32) for half in range(S5_PARTS)]
    y = jnp.concatenate(outs, axis=1) + d_ref[...] * u
    for j in range(n_col):
        ut_ref[j] = y[:, j * LANES:(j + 1) * LANES]
    for b in range(S5_BATCH):
        for j in range(n_col):
            col = b * S5_WIDTH + j * LANES
            y_ref[:, col:col + LANES] = ut_ref[j, pl.ds(b, frames, stride=S5_BATCH), :]


def _s5(u_lb, wcat, cmat, lam2, d_row, rows):
    seq = u_lb.shape[0]
    frames = rows // S5_BATCH
    hs = S5_HALF_STATE
    const2 = lambda i: (0, 0)
    const3 = lambda i: (0, 0, 0)
    return pl.pallas_call(
        functools.partial(_s5_kernel, rows=rows),
        grid=(seq // frames,),
        in_specs=[pl.BlockSpec((frames, S5_BATCH * S5_WIDTH), lambda i: (i, 0)),
                  pl.BlockSpec((S5_PARTS, 2 * S5_HALF, 2 * hs), const3),
                  pl.BlockSpec((S5_PARTS, 2 * hs, S5_HALF), const3),
                  pl.BlockSpec((SCAN_ROWS, 2 * S5_PARTS * hs), const2),
                  pl.BlockSpec((1, S5_WIDTH), const2)],
        out_specs=pl.BlockSpec((frames, S5_BATCH * S5_WIDTH), lambda i: (i, 0)),
        out_shape=jax.ShapeDtypeStruct((seq, S5_BATCH * S5_WIDTH), F32),
        scratch_shapes=[pltpu.VMEM((S5_WIDTH // LANES, rows, LANES), F32),
                        pltpu.VMEM((SCAN_ROWS, S5_WIDTH), F32),
                        pltpu.VMEM((SCAN_ROWS, 2 * S5_PARTS * hs), F32),
                        pltpu.VMEM((rows, 2 * S5_PARTS * hs), F32)],
        compiler_params=pltpu.CompilerParams(dimension_semantics=("arbitrary",),
                                             vmem_limit_bytes=VMEM_LIMIT_BYTES),
        name="s5",
    )(u_lb, wcat, cmat, lam2, d_row)


def _s5_params(lam_re, lam_im, log_dt, b_re, b_im, c_re, c_im):
    dt = jnp.exp(log_dt)[:, None]
    mag = jnp.exp(lam_re * dt)
    ang = lam_im * dt
    ab_re = mag * jnp.cos(ang)
    ab_im = mag * jnp.sin(ang)
    den = lam_re * lam_re + lam_im * lam_im
    nr = ab_re - 1.0
    f_re = (nr * lam_re + ab_im * lam_im) / den
    f_im = (ab_im * lam_re - nr * lam_im) / den
    bb_re = f_re[..., None] * b_re - f_im[..., None] * b_im
    bb_im = f_re[..., None] * b_im + f_im[..., None] * b_re

    gh = S5_GROUPS // S5_PARTS
    eye = jnp.eye(gh, dtype=F32)

    def in_half(bb):
        return jnp.einsum('gph,gk->ghkp', bb, eye).reshape(gh * S5_GROUP_DIM, gh * S5_STATE)

    def out_half(cc):
        return jnp.einsum('ghp,gk->gpkh', cc, eye).reshape(gh * S5_STATE, gh * S5_GROUP_DIM)

    lb_re = ab_re[..., None] * bb_re - ab_im[..., None] * bb_im
    lb_im = ab_re[..., None] * bb_im + ab_im[..., None] * bb_re
    halves = [slice(i * gh, (i + 1) * gh) for i in range(S5_PARTS)]
    wcat = jnp.stack([jnp.concatenate(
        [jnp.concatenate([in_half(bb_re[s]), in_half(bb_im[s])], axis=1),
         jnp.concatenate([in_half(lb_re[s]), in_half(lb_im[s])], axis=1)], axis=0)
        for s in halves])
    cmat = jnp.stack([jnp.concatenate([out_half(c_re[s]), -out_half(c_im[s])], axis=0)
                      for s in halves])

    l2_re = (mag * mag * jnp.cos(2.0 * ang)).reshape(S5_PARTS, gh * S5_STATE)
    l2_im = (mag * mag * jnp.sin(2.0 * ang)).reshape(S5_PARTS, gh * S5_STATE)
    lam2 = jnp.concatenate([part for i in range(S5_PARTS) for part in (l2_re[i], l2_im[i])])[None, :]
    lam2 = jnp.broadcast_to(lam2, (SCAN_ROWS, 2 * S5_GROUPS * S5_STATE))
    return wcat.astype(BF16), cmat.astype(BF16), lam2


def _out_kernel(x_ref, ya_ref, ga_ref, ys_ref, gb_ref, m_ref, wglu_ref, bglu_ref, pa_ref, pb_ref,
                wout_ref, fg_ref, o_ref):
    y_a = ya_ref[...] * jax.nn.silu(ga_ref[...].astype(F32))
    proj_a = jnp.dot(y_a.astype(BF16), pa_ref[...], preferred_element_type=F32)
    ys = jax.nn.gelu(ys_ref[...])
    glu = jnp.dot(ys.astype(BF16), wglu_ref[...], preferred_element_type=F32) + bglu_ref[...]
    y_b = glu[:, :S5_WIDTH] * jax.nn.sigmoid(glu[:, S5_WIDTH:]) * jax.nn.silu(gb_ref[...].astype(F32))
    proj_b = jnp.dot(y_b.astype(BF16), pb_ref[...], preferred_element_type=F32)
    m = m_ref[...].astype(F32)
    merged = jax.nn.sigmoid(m[:, :D_MODEL]) * proj_a + jax.nn.sigmoid(m[:, D_MODEL:]) * proj_b
    x_new = x_ref[...] + jnp.dot(merged.astype(BF16), wout_ref[...], preferred_element_type=F32)
    o_ref[...] = _rms_norm(x_new, fg_ref[...])


def _out(x2, ya, ga, ys_lb, gb, m, w_glu, b_glu, p_a, p_b, w_out, final_g, tm):
    t = x2.shape[0]
    const = lambda i: (0, 0)
    tile = lambda w: pl.BlockSpec((tm, w), lambda i: (i, 0))
    full = lambda a: pl.BlockSpec(a.shape, const)
    return pl.pallas_call(
        _out_kernel,
        grid=(t // tm,),
        in_specs=[tile(D_MODEL), tile(RWKV_WIDTH), tile(RWKV_WIDTH),
                  _frame_batch_spec(tm, ys_lb.shape[0] // tm), tile(S5_WIDTH),
                  tile(2 * D_MODEL), full(w_glu), full(b_glu), full(p_a), full(p_b), full(w_out),
                  full(final_g)],
        out_specs=tile(D_MODEL),
        out_shape=jax.ShapeDtypeStruct((t, D_MODEL), F32),
        compiler_params=pltpu.CompilerParams(dimension_semantics=("arbitrary",),
                                             vmem_limit_bytes=VMEM_LIMIT_BYTES),
        name="merge_out",
    )(x2, ya, ga, ys_lb, gb, m, w_glu, b_glu, p_a, p_b, w_out, final_g)


def kernel(x, norm_g, w_in, mu_shift, w0, w_up, a0, a_up, k_k, k_a, r_k, lnx_g, lnx_b, lam_re, lam_im,
           log_dt, b_re, b_im, c_re, c_im, d_skip, w_glu, b_glu, p_a, p_b, w_out, final_g):
    bsz, seq, _ = x.shape
    assert norm_g.shape[0] == 1, "single-layer block: the final RMSNorm is fused into the output kernel"
    tm = min(512, seq)
    tt_wkv = min(256, seq)
    assert bsz == S5_BATCH, "the S5 kernel packs 2 frames x 4 batch rows per sublane block"
    rows_s5 = min(1024, seq * bsz)
    x2 = x.reshape(bsz * seq, D_MODEL)
    row = lambda p: p.reshape(1, -1).astype(F32)
    l = 0
    (rt, at, bk, v_sw, gc, bonus_sw, ga, u_lb, gb, m) = _prep(
        x2, row(norm_g[l]), w_in[l].astype(BF16), row(mu_shift[l]), row(w0[l]), w_up[l], row(a0[l]), a_up[l],
        row(k_k[l]), row(k_a[l]), row(r_k[l]), tt_wkv, bsz, seq)
    swap_row = lambda p: row(p).reshape(HEADS // 2, 2, HEAD_DIM)[:, ::-1].reshape(1, RWKV_WIDTH)
    ya = _wkv(rt, at, bk, v_sw, gc, bonus_sw, swap_row(lnx_g[l]), swap_row(lnx_b[l]), bsz, seq,
              min(2 * tt_wkv, seq))
    wcat, cmat, lam2 = _s5_params(lam_re[l], lam_im[l], log_dt[l], b_re[l], b_im[l], c_re[l], c_im[l])
    ys_lb = _s5(u_lb, wcat, cmat, lam2, row(d_skip[l]), rows_s5)
    out = _out(x2, ya, ga, ys_lb, gb, m, w_glu[l].astype(BF16), row(b_glu[l]), p_a[l].astype(BF16),
               p_b[l].astype(BF16), w_out[l].astype(BF16), row(final_g), tm)
    return out.reshape(bsz, seq, D_MODEL)
```

```python
import functools
import math

import jax
import jax.numpy as jnp
from jax import lax
from jax.experimental import pallas as pl
from jax.experimental.pallas import tpu as pltpu

F32 = jnp.float32
BF16 = jnp.bfloat16

D_MODEL = 1024
HEADS = 16
HEAD_DIM = 64
RWKV_WIDTH = HEADS * HEAD_DIM
LORA = 64
S5_GROUPS = 32
S5_GROUP_DIM = 16
S5_WIDTH = S5_GROUPS * S5_GROUP_DIM
S5_STATE = 64
SHIFT_WIDTH = 3 * RWKV_WIDTH + 2 * LORA
IN_WIDTH = SHIFT_WIDTH + RWKV_WIDTH + 2 * S5_WIDTH + 2 * D_MODEL
RMS_EPS = 1e-6
LNX_EPS = 64e-5

LANES = 128
SUBLANES = 8
CHUNK = 64
PAIR = 2 * HEAD_DIM
CHUNK_GROUP = 2
PROJ_SLICE = 512
SEG_TILE = 256
S5_PARTS = 4
S5_HALF = S5_WIDTH // S5_PARTS
S5_HALF_STATE = (S5_GROUPS // S5_PARTS) * S5_STATE
SCAN_ROWS = SUBLANES
S5_BATCH = 4

VMEM_LIMIT_BYTES = 56 * 1024 * 1024


def _split_bf16(x):
    hi = x.astype(BF16)
    lo = (x - hi.astype(F32)).astype(BF16)
    return hi, lo


def _bf16_dot(a, b):
    return jnp.dot(a.astype(BF16), b.astype(BF16), preferred_element_type=F32)


def _iota(shape, dim):
    return lax.broadcasted_iota(jnp.int32, shape, dim)


def _seg_sum(x, ones_bd):
    xb = x.astype(BF16)
    cols = [jnp.dot(xb[:, j * SEG_TILE:(j + 1) * SEG_TILE], ones_bd, preferred_element_type=F32)
            for j in range(x.shape[1] // SEG_TILE)]
    return jnp.concatenate(cols, axis=1)


def _swap_halves(x):
    n = x.shape[1]
    low = (_iota(x.shape, 1) % PAIR) < HEAD_DIM
    return jnp.where(low, pltpu.roll(x, n - HEAD_DIM, 1), pltpu.roll(x, HEAD_DIM, 1))


def _rms_norm(x, g):
    return x * lax.rsqrt(jnp.mean(x * x, axis=-1, keepdims=True) + RMS_EPS) * g


_IN_SPLITS = (0, SHIFT_WIDTH, SHIFT_WIDTH + RWKV_WIDTH, SHIFT_WIDTH + RWKV_WIDTH + S5_WIDTH,
              SHIFT_WIDTH + RWKV_WIDTH + 2 * S5_WIDTH, IN_WIDTH)


def _seg_matrix(swapped):
    seg_i = _iota((SEG_TILE, SEG_TILE), 0) // HEAD_DIM
    seg_j = _iota((SEG_TILE, SEG_TILE), 1) // HEAD_DIM
    if swapped:
        seg_j = seg_j + 1 - 2 * (seg_j % 2)
    return jnp.where(seg_i == seg_j, 1.0, 0.0).astype(BF16)


def _prep_kernel(x_ref, g_ref, w_ref, mu_ref, w0_ref, wup_ref, a0_ref, aup_ref, kk_ref, ka_ref, rk_ref,
                 rt_ref, at_ref, bk_ref, v_ref, gc_ref, bonus_ref, ga_ref, u_ref, gb_ref, m_ref,
                 prev_ref, *, tm, n_t):
    n_chunks = tm // CHUNK

    @pl.when(pl.program_id(0) % n_t == 0)
    def _():
        prev_ref[...] = jnp.zeros_like(prev_ref)

    h = _rms_norm(x_ref[...], g_ref[...]).astype(BF16)
    proj = lambda lo, hi: jnp.dot(h, w_ref[:, lo:hi], preferred_element_type=F32)

    out_cols = []
    for o_ref, lo, hi in zip((ga_ref, u_ref, gb_ref, m_ref), _IN_SPLITS[1:-1], _IN_SPLITS[2:]):
        for c0 in range(lo, hi, PROJ_SLICE):
            out_cols.append((o_ref, c0 - lo, c0, min(c0 + PROJ_SLICE, hi)))
    out_cols = iter(out_cols)

    def project(n=1):
        for _ in range(n):
            item = next(out_cols, None)
            if item is not None:
                o_ref, at_col, lo, hi = item
                o_ref[:, at_col:at_col + hi - lo] = proj(lo, hi).astype(o_ref.dtype)

    row0 = _iota((tm, 1), 0) == 0

    def shifted(z, lo, hi):
        z_prev = jnp.where(row0, prev_ref[:, lo:hi], pltpu.roll(z, 1, 0))
        return z + mu_ref[:, lo:hi] * (z_prev - z), z[tm - 1:tm, :]

    x_lo = 3 * RWKV_WIDTH
    zx, last_x = shifted(proj(x_lo, SHIFT_WIDTH), x_lo, SHIFT_WIDTH)
    zk_raw = proj(RWKV_WIDTH, 2 * RWKV_WIDTH)
    wl = w0_ref[...] + _bf16_dot(jnp.tanh(zx[:, :LORA]), wup_ref[...])
    logw = -math.exp(-0.5) * jax.nn.sigmoid(wl)
    iclr = jax.nn.sigmoid(a0_ref[...] + _bf16_dot(zx[:, LORA:], aup_ref[...]))
    zr_raw = proj(0, RWKV_WIDTH)
    ti = _iota((tm, tm), 0)
    si = _iota((tm, tm), 1)
    tril_bd = jnp.where(((ti // CHUNK) == (si // CHUNK)) & (si <= ti), 1.0, 0.0).astype(BF16)
    lw_hi, lw_lo = _split_bf16(logw)
    cum = (jnp.dot(tril_bd, lw_hi, preferred_element_type=F32)
           + jnp.dot(tril_bd, lw_lo, preferred_element_type=F32))
    k, last_k = shifted(zk_raw, RWKV_WIDTH, 2 * RWKV_WIDTH)
    zv_raw = proj(2 * RWKV_WIDTH, 3 * RWKV_WIDTH)
    kk = k * kk_ref[...]
    kk = kk * lax.rsqrt(jnp.maximum(_seg_sum(kk * kk, _seg_matrix(False)), 1e-24))
    k_h = k * (1.0 + (iclr - 1.0) * ka_ref[...])
    b = kk * iclr
    project(2)
    by_chunk = lambda t: t.reshape(n_chunks, CHUNK, RWKV_WIDTH)
    g_c = jnp.exp(by_chunk(cum)[:, CHUNK - 1:CHUNK, :])
    gc_ref[...] = jnp.broadcast_to(g_c, (n_chunks, SUBLANES, RWKV_WIDTH)).reshape(n_chunks * SUBLANES, RWKV_WIDTH)
    r, last_r = shifted(zr_raw, 0, RWKV_WIDTH)
    rt_ref[...] = r * jnp.exp(cum)
    project(2)
    at_ref[...] = -kk * jnp.exp(cum - logw)
    bonus_ref[...] = _seg_sum(r * k_h * rk_ref[...], _seg_matrix(True))
    project(2)
    v, last_v = shifted(zv_raw, 2 * RWKV_WIDTH, 3 * RWKV_WIDTH)
    v_ref[...] = _swap_halves(v)
    prev_ref[...] = jnp.concatenate([last_r, last_k, last_v, last_x], axis=1)
    project(2)
    g_inv = jnp.exp(-cum)
    b_t = b * g_inv
    k_t = k_h * g_inv
    project(2)
    for c in range(n_chunks):
        for p in range(HEADS // 2):
            blk = (slice(c * CHUNK, (c + 1) * CHUNK), slice(p * PAIR, (p + 1) * PAIR))
            dst = slice((c * (HEADS // 2) + p) * PAIR, (c * (HEADS // 2) + p + 1) * PAIR)
            bk_ref[dst, :] = jnp.concatenate([b_t[blk], k_t[blk]], axis=0).T.astype(BF16)
        project(1)
    project(64)


def _frame_batch_spec(tm, n_t):
    return pl.BlockSpec((tm, S5_WIDTH), lambda i: (i % n_t, i // n_t))


def _prep(x2, norm_g, w_in_bf16, mu, w0, w_up, a0, a_up, k_k, k_a, r_k, tm, bsz, seq):
    t = x2.shape[0]
    n_t = seq // tm
    const = lambda i: (0, 0)
    rowspec = lambda w: pl.BlockSpec((1, w), const)
    tile = lambda rows, w: pl.BlockSpec((rows, w), lambda i: (i, 0))
    pair_rows = tm // CHUNK * (HEADS // 2) * PAIR
    wide = jax.ShapeDtypeStruct((t, RWKV_WIDTH), F32)
    key_major = jax.ShapeDtypeStruct((t // CHUNK * (HEADS // 2) * PAIR, PAIR), BF16)
    out_shape = [wide, wide, key_major, wide,
                 jax.ShapeDtypeStruct((t // CHUNK * SUBLANES, RWKV_WIDTH), F32), wide,
                 jax.ShapeDtypeStruct((t, RWKV_WIDTH), BF16),
                 jax.ShapeDtypeStruct((seq, bsz * S5_WIDTH), F32),
                 jax.ShapeDtypeStruct((t, S5_WIDTH), BF16),
                 jax.ShapeDtypeStruct((t, 2 * D_MODEL), BF16)]
    out_specs = [tile(tm, RWKV_WIDTH), tile(tm, RWKV_WIDTH), tile(pair_rows, PAIR),
                 tile(tm, RWKV_WIDTH), tile(tm // CHUNK * SUBLANES, RWKV_WIDTH), tile(tm, RWKV_WIDTH),
                 tile(tm, RWKV_WIDTH), _frame_batch_spec(tm, n_t), tile(tm, S5_WIDTH), tile(tm, 2 * D_MODEL)]
    return pl.pallas_call(
        functools.partial(_prep_kernel, tm=tm, n_t=n_t),
        grid=(t // tm,),
        in_specs=[pl.BlockSpec((tm, D_MODEL), lambda i: (i, 0)),
                  rowspec(D_MODEL),
                  pl.BlockSpec((D_MODEL, IN_WIDTH), const, pipeline_mode=pl.Buffered(1)),
                  rowspec(SHIFT_WIDTH), rowspec(RWKV_WIDTH),
                  pl.BlockSpec((LORA, RWKV_WIDTH), const),
                  rowspec(RWKV_WIDTH),
                  pl.BlockSpec((LORA, RWKV_WIDTH), const),
                  rowspec(RWKV_WIDTH), rowspec(RWKV_WIDTH), rowspec(RWKV_WIDTH)],
        out_specs=out_specs,
        out_shape=out_shape,
        scratch_shapes=[pltpu.VMEM((1, SHIFT_WIDTH), F32)],
        compiler_params=pltpu.CompilerParams(dimension_semantics=("arbitrary",),
                                             vmem_limit_bytes=VMEM_LIMIT_BYTES),
        name="inproj_prep",
    )(x2, norm_g, w_in_bf16, mu, w0, w_up, a0, a_up, k_k, k_a, r_k)


def _wkv_kernel(rt_ref, at_ref, bk_ref, v_ref, gc_ref, bonus_ref, lng_ref, lnb_ref, y_ref,
                h_ref, yc_ref, *, tt):
    n_chunks = tt // CHUNK

    @pl.when(pl.program_id(1) == 0)
    def _():
        h_ref[...] = jnp.zeros_like(h_ref)

    gi = _iota((2 * CHUNK, 2 * CHUNK), 0)
    gj = _iota((2 * CHUNK, 2 * CHUNK), 1) % CHUNK
    g_mask = gj <= jnp.where(gi < CHUNK, gi, gi - (CHUNK + 1))
    lane = _iota((CHUNK, PAIR), 1)
    low = lane < HEAD_DIM
    own = (low, jnp.logical_not(low))
    zeros_cp = jnp.zeros((CHUNK, PAIR), F32)
    cast = lambda t: t.astype(BF16)
    n_pairs = HEADS // 2

    def block(c, size):
        return pl.ds(c * size, size) if isinstance(c, int) else pl.ds(pl.multiple_of(c * size, size), size)

    def phase_a(c0):
        chunks = tuple(range(CHUNK_GROUP))
        cps = [(c, p) for c in chunks for p in range(n_pairs)]
        chs = [(c, h) for c in chunks for h in range(HEADS)]
        ld = lambda ref, c, p: ref[block(c0 + c, CHUNK), p * PAIR:(p + 1) * PAIR]
        rt = {(c, p): ld(rt_ref, c, p) for c, p in cps}
        at = {(c, p): ld(at_ref, c, p) for c, p in cps}
        vs = {(c, p): ld(v_ref, c, p) for c, p in cps}

        key_major = lambda ref, c, p: ref[block((c0 + c) * n_pairs + p, PAIR), :]
        bk = {(c, p): key_major(bk_ref, c, p) for c, p in cps}

        g, x, vo, m = {}, {}, {}, {}
        for c, h in chs:
            p, s = divmod(h, 2)
            lhs = jnp.concatenate([jnp.where(own[s], rt[c, p], 0.0), jnp.where(own[s], at[c, p], 0.0)], axis=0)
            g[c, h] = cast(jnp.where(g_mask, _bf16_dot(cast(lhs), bk[c, p]), 0.0))
        for c, h in chs:
            p, s = divmod(h, 2)
            vo[c, h] = cast(jnp.where(own[s], 0.0, vs[c, p]))
            vak = _bf16_dot(g[c, h][CHUNK:, :], jnp.concatenate([cast(zeros_cp), vo[c, h]], axis=0))
            x[c, h] = jnp.where(own[s], at[c, p], 0.0) + vak
            m[c, h] = g[c, h][CHUNK:, :CHUNK]
        for step in range(5):
            for ch in chs:
                zz = _bf16_dot(m[ch], jnp.concatenate([cast(x[ch]), m[ch]], axis=1))
                x[ch] = x[ch] + zz[:, :PAIR]
                m[ch] = cast(zz[:, PAIR:])
        for ch in chs:
            x[ch] = x[ch] + _bf16_dot(m[ch], cast(x[ch]))
        o9 = {}
        for c, h in chs:
            p, s = divmod(h, 2)
            lhs = jnp.concatenate([g[c, h][:CHUNK, :], bk[c, p][s * CHUNK:(s + 1) * CHUNK, :]], axis=0)
            o9[c, h] = _bf16_dot(lhs, jnp.concatenate([cast(x[c, h]), vo[c, h]], axis=0))
        out = {}
        for c, p in cps:
            t0, t1 = o9[c, 2 * p][:CHUNK], o9[c, 2 * p + 1][:CHUNK]
            b0, b1 = o9[c, 2 * p][CHUNK:], o9[c, 2 * p + 1][CHUNK:]
            r_p = rt[c, p] + jnp.where(low, t0, t1)
            y0 = jnp.where(low, t1, t0)
            phi = jnp.concatenate([jnp.where(low, b0, 0.0), jnp.where(low, 0.0, b1)], axis=0)
            hk = jnp.concatenate([jnp.where(low, 0.0, b0), jnp.where(low, b1, 0.0)], axis=0)
            out[c, p] = (cast(jnp.concatenate([r_p, phi], axis=0)), y0, hk)
        return out

    def phase_b(c0, c, prepared, states):
        new_states = []
        for p in range(n_pairs):
            lhs, y0, hk = prepared[c, p]
            gc = gc_ref[block(c0 + c, SUBLANES), p * PAIR:(p + 1) * PAIR][0:1, :]
            g_col = jnp.broadcast_to(gc, (PAIR, PAIR)).T
            zz = _bf16_dot(lhs, states[p])
            yc_ref[block(c0 + c, CHUNK), p * PAIR:(p + 1) * PAIR] = zz[:CHUNK] + y0
            new_states.append(g_col * (states[p] + (zz[CHUNK:] + hk)))
        return new_states

    states = [h_ref[p] for p in range(n_pairs)]
    for c0 in range(0, n_chunks, CHUNK_GROUP):
        prepared = phase_a(c0)
        for c in range(CHUNK_GROUP):
            states = phase_b(c0, c, prepared, states)
    for p in range(n_pairs):
        h_ref[p] = states[p]

    ones_bd = _seg_matrix(False)
    y = yc_ref[...]
    mean = _seg_sum(y, ones_bd) * (1.0 / HEAD_DIM)
    yc = y - mean
    var = _seg_sum(yc * yc, ones_bd) * (1.0 / HEAD_DIM)
    yn = yc * lax.rsqrt(var + LNX_EPS) * lng_ref[...] + lnb_ref[...]
    y_ref[...] = _swap_halves(yn + bonus_ref[...] * v_ref[...])


def _wkv(rt, at, bk, v_sw, gc, bonus_sw, lnx_g_sw, lnx_b_sw, bsz, seq, tt):
    n_t = seq // tt
    const = lambda b, i: (0, 0)
    tile = lambda rows, w: pl.BlockSpec((rows, w), lambda b, i: (b * n_t + i, 0))
    pair_rows = tt // CHUNK * (HEADS // 2) * PAIR
    return pl.pallas_call(
        functools.partial(_wkv_kernel, tt=tt),
        grid=(bsz, n_t),
        in_specs=[tile(tt, RWKV_WIDTH), tile(tt, RWKV_WIDTH), tile(pair_rows, PAIR),
                  tile(tt, RWKV_WIDTH), tile(tt // CHUNK * SUBLANES, RWKV_WIDTH), tile(tt, RWKV_WIDTH),
                  pl.BlockSpec((1, RWKV_WIDTH), const), pl.BlockSpec((1, RWKV_WIDTH), const)],
        out_specs=tile(tt, RWKV_WIDTH),
        out_shape=jax.ShapeDtypeStruct((bsz * seq, RWKV_WIDTH), F32),
        scratch_shapes=[pltpu.VMEM((HEADS // 2, PAIR, PAIR), F32),
                        pltpu.VMEM((tt, RWKV_WIDTH), F32)],
        compiler_params=pltpu.CompilerParams(dimension_semantics=("arbitrary", "arbitrary"),
                                             vmem_limit_bytes=VMEM_LIMIT_BYTES),
        name="wkv7",
    )(rt, at, bk, v_sw, gc, bonus_sw, lnx_g_sw, lnx_b_sw)


def _s5_kernel(u_ref, w_ref, cm_ref, lam2_ref, d_ref, y_ref, ut_ref, prev_ref, carry_ref, st_ref, *, rows):
    @pl.when(pl.program_id(0) == 0)
    def _():
        prev_ref[...] = jnp.zeros_like(prev_ref)
        carry_ref[...] = jnp.zeros_like(carry_ref)

    frames = rows // S5_BATCH
    n_col = S5_WIDTH // LANES
    for b in range(S5_BATCH):
        for j in range(n_col):
            col = b * S5_WIDTH + j * LANES
            ut_ref[j, pl.ds(b, frames, stride=S5_BATCH), :] = u_ref[:, col:col + LANES]
    u = jnp.concatenate([ut_ref[j] for j in range(n_col)], axis=1)
    hs = S5_HALF_STATE
    shifted = pltpu.roll(u, S5_BATCH, 0)
    row8 = _iota((SCAN_ROWS, S5_WIDTH), 0)
    head = jnp.where(row8 < S5_BATCH, pltpu.roll(prev_ref[...], S5_BATCH, 0), shifted[:SCAN_ROWS])
    u_prev = jnp.concatenate([head, shifted[SCAN_ROWS:]], axis=0)
    prev_ref[...] = u[rows - SCAN_ROWS:, :]
    ub = u.astype(BF16)
    upb = u_prev.astype(BF16)
    for half in range(S5_PARTS):
        cols = slice(half * S5_HALF, (half + 1) * S5_HALF)
        st_ref[:, half * 2 * hs:(half + 1) * 2 * hs] = jnp.dot(
            jnp.concatenate([ub[:, cols], upb[:, cols]], axis=1), w_ref[half], preferred_element_type=F32)

    n_blocks = rows // SCAN_ROWS
    for half in range(S5_PARTS):
        re_l = pl.ds(half * 2 * hs, hs)
        im_l = pl.ds(half * 2 * hs + hs, hs)
        l_re = lam2_ref[:, re_l]
        l_im = lam2_ref[:, im_l]

        def block(i, carry, re_l=re_l, im_l=im_l, l_re=l_re, l_im=l_im):
            c_re, c_im = carry
            blk = pl.ds(i * SCAN_ROWS, SCAN_ROWS)
            x_re = st_ref[blk, re_l] + (l_re * c_re - l_im * c_im)
            x_im = st_ref[blk, im_l] + (l_re * c_im + l_im * c_re)
            st_ref[blk, re_l] = x_re
            st_ref[blk, im_l] = x_im
            return x_re, x_im

        carry = (carry_ref[:, re_l], carry_ref[:, im_l])
        for i in range(n_blocks):
            carry = block(i, carry)
        c_re, c_im = carry
        carry_ref[:, re_l] = c_re
        carry_ref[:, im_l] = c_im

    outs = [jnp.dot(st_ref[:, half * 2 * hs:(half + 1) * 2 * hs].astype(BF16), cm_ref[half],
                    preferred_element_type=F32) for half in range(S5_PARTS)]
    y = jnp.concatenate(outs, axis=1) + d_ref[...] * u
    for j in range(n_col):
        ut_ref[j] = y[:, j * LANES:(j + 1) * LANES]
    for b in range(S5_BATCH):
        for j in range(n_col):
            col = b * S5_WIDTH + j * LANES
            y_ref[:, col:col + LANES] = ut_ref[j, pl.ds(b, frames, stride=S5_BATCH), :]


def _s5(u_lb, wcat, cmat, lam2, d_row, rows):
    seq = u_lb.shape[0]
    frames = rows // S5_BATCH
    hs = S5_HALF_STATE
    const2 = lambda i: (0, 0)
    const3 = lambda i: (0, 0, 0)
    return pl.pallas_call(
        functools.partial(_s5_kernel, rows=rows),
        grid=(seq // frames,),
        in_specs=[pl.BlockSpec((frames, S5_BATCH * S5_WIDTH), lambda i: (i, 0)),
                  pl.BlockSpec((S5_PARTS, 2 * S5_HALF, 2 * hs), const3, pipeline_mode=pl.Buffered(1)),
                  pl.BlockSpec((S5_PARTS, 2 * hs, S5_HALF), const3, pipeline_mode=pl.Buffered(1)),
                  pl.BlockSpec((SCAN_ROWS, 2 * S5_PARTS * hs), const2, pipeline_mode=pl.Buffered(1)),
                  pl.BlockSpec((1, S5_WIDTH), const2, pipeline_mode=pl.Buffered(1))],
        out_specs=pl.BlockSpec((frames, S5_BATCH * S5_WIDTH), lambda i: (i, 0)),
        out_shape=jax.ShapeDtypeStruct((seq, S5_BATCH * S5_WIDTH), F32),
        scratch_shapes=[pltpu.VMEM((S5_WIDTH // LANES, rows, LANES), F32),
                        pltpu.VMEM((SCAN_ROWS, S5_WIDTH), F32),
                        pltpu.VMEM((SCAN_ROWS, 2 * S5_PARTS * hs), F32),
                        pltpu.VMEM((rows, 2 * S5_PARTS * hs), F32)],
        compiler_params=pltpu.CompilerParams(dimension_semantics=("arbitrary",),
                                             vmem_limit_bytes=VMEM_LIMIT_BYTES),
        name="s5",
    )(u_lb, wcat, cmat, lam2, d_row)


def _s5_params(lam_re, lam_im, log_dt, b_re, b_im, c_re, c_im):
    dt = jnp.exp(log_dt)[:, None]
    mag = jnp.exp(lam_re * dt)
    ang = lam_im * dt
    ab_re = mag * jnp.cos(ang)
    ab_im = mag * jnp.sin(ang)
    den = lam_re * lam_re + lam_im * lam_im
    nr = ab_re - 1.0
    f_re = (nr * lam_re + ab_im * lam_im) / den
    f_im = (ab_im * lam_re - nr * lam_im) / den
    bb_re = f_re[..., None] * b_re - f_im[..., None] * b_im
    bb_im = f_re[..., None] * b_im + f_im[..., None] * b_re

    gh = S5_GROUPS // S5_PARTS
    eye = jnp.eye(gh, dtype=F32)

    def in_half(bb):
        return jnp.einsum('gph,gk->ghkp', bb, eye).reshape(gh * S5_GROUP_DIM, gh * S5_STATE)

    def out_half(cc):
        return jnp.einsum('ghp,gk->gpkh', cc, eye).reshape(gh * S5_STATE, gh * S5_GROUP_DIM)

    lb_re = ab_re[..., None] * bb_re - ab_im[..., None] * bb_im
    lb_im = ab_re[..., None] * bb_im + ab_im[..., None] * bb_re
    halves = [slice(i * gh, (i + 1) * gh) for i in range(S5_PARTS)]
    wcat = jnp.stack([jnp.concatenate(
        [jnp.concatenate([in_half(bb_re[s]), in_half(bb_im[s])], axis=1),
         jnp.concatenate([in_half(lb_re[s]), in_half(lb_im[s])], axis=1)], axis=0)
        for s in halves])
    cmat = jnp.stack([jnp.concatenate([out_half(c_re[s]), -out_half(c_im[s])], axis=0)
                      for s in halves])

    l2_re = (mag * mag * jnp.cos(2.0 * ang)).reshape(S5_PARTS, gh * S5_STATE)
    l2_im = (mag * mag * jnp.sin(2.0 * ang)).reshape(S5_PARTS, gh * S5_STATE)
    lam2 = jnp.concatenate([part for i in range(S5_PARTS) for part in (l2_re[i], l2_im[i])])[None, :]
    lam2 = jnp.broadcast_to(lam2, (SCAN_ROWS, 2 * S5_GROUPS * S5_STATE))
    return wcat.astype(BF16), cmat.astype(BF16), lam2


def _out_kernel(x_ref, ya_ref, ga_ref, ys_ref, gb_ref, m_ref, wglu_ref, bglu_ref, pa_ref, pb_ref,
                wout_ref, fg_ref, o_ref):
    y_a = ya_ref[...] * jax.nn.silu(ga_ref[...].astype(F32))
    proj_a = jnp.dot(y_a.astype(BF16), pa_ref[...], preferred_element_type=F32)
    ys = jax.nn.gelu(ys_ref[...])
    glu = jnp.dot(ys.astype(BF16), wglu_ref[...], preferred_element_type=F32) + bglu_ref[...]
    y_b = glu[:, :S5_WIDTH] * jax.nn.sigmoid(glu[:, S5_WIDTH:]) * jax.nn.silu(gb_ref[...].astype(F32))
    proj_b = jnp.dot(y_b.astype(BF16), pb_ref[...], preferred_element_type=F32)
    m = m_ref[...].astype(F32)
    merged = jax.nn.sigmoid(m[:, :D_MODEL]) * proj_a + jax.nn.sigmoid(m[:, D_MODEL:]) * proj_b
    x_new = x_ref[...] + jnp.dot(merged.astype(BF16), wout_ref[...], preferred_element_type=F32)
    o_ref[...] = _rms_norm(x_new, fg_ref[...])


def _out(x2, ya, ga, ys_lb, gb, m, w_glu, b_glu, p_a, p_b, w_out, final_g, tm):
    t = x2.shape[0]
    const = lambda i: (0, 0)
    tile = lambda w: pl.BlockSpec((tm, w), lambda i: (i, 0))
    full = lambda a: pl.BlockSpec(a.shape, const, pipeline_mode=pl.Buffered(1))
    return pl.pallas_call(
        _out_kernel,
        grid=(t // tm,),
        in_specs=[tile(D_MODEL), tile(RWKV_WIDTH), tile(RWKV_WIDTH),
                  _frame_batch_spec(tm, ys_lb.shape[0] // tm), tile(S5_WIDTH),
                  tile(2 * D_MODEL), full(w_glu), full(b_glu), full(p_a), full(p_b), full(w_out),
                  full(final_g)],
        out_specs=tile(D_MODEL),
        out_shape=jax.ShapeDtypeStruct((t, D_MODEL), F32),
        compiler_params=pltpu.CompilerParams(dimension_semantics=("arbitrary",),
                                             vmem_limit_bytes=VMEM_LIMIT_BYTES),
        name="merge_out",
    )(x2, ya, ga, ys_lb, gb, m, w_glu, b_glu, p_a, p_b, w_out, final_g)


def kernel(x, norm_g, w_in, mu_shift, w0, w_up, a0, a_up, k_k, k_a, r_k, lnx_g, lnx_b, lam_re, lam_im,
           log_dt, b_re, b_im, c_re, c_im, d_skip, w_glu, b_glu, p_a, p_b, w_out, final_g):
    bsz, seq, _ = x.shape
    assert norm_g.shape[0] == 1, "single-layer block: the final RMSNorm is fused into the output kernel"
    tm = min(512, seq)
    tt_wkv = min(256, seq)
    assert bsz == S5_BATCH, "the S5 kernel packs 2 frames x 4 batch rows per sublane block"
    rows_s5 = min(1024, seq * bsz)
    x2 = x.reshape(bsz * seq, D_MODEL)
    row = lambda p: p.reshape(1, -1).astype(F32)
    l = 0
    (rt, at, bk, v_sw, gc, bonus_sw, ga, u_lb, gb, m) = _prep(
        x2, row(norm_g[l]), w_in[l].astype(BF16), row(mu_shift[l]), row(w0[l]), w_up[l], row(a0[l]), a_up[l],
        row(k_k[l]), row(k_a[l]), row(r_k[l]), tt_wkv, bsz, seq)
    swap_row = lambda p: row(p).reshape(HEADS // 2, 2, HEAD_DIM)[:, ::-1].reshape(1, RWKV_WIDTH)
    ya = _wkv(rt, at, bk, v_sw, gc, bonus_sw, swap_row(lnx_g[l]), swap_row(lnx_b[l]), bsz, seq,
              min(2 * tt_wkv, seq))
    wcat, cmat, lam2 = _s5_params(lam_re[l], lam_im[l], log_dt[l], b_re[l], b_im[l], c_re[l], c_im[l])
    ys_lb = _s5(u_lb, wcat, cmat, lam2, row(d_skip[l]), rows_s5)
    out = _out(x2, ya, ga, ys_lb, gb, m, w_glu[l].astype(BF16), row(b_glu[l]), p_a[l].astype(BF16),
               p_b[l].astype(BF16), w_out[l].astype(BF16), row(final_g), tm)
    return out.reshape(bsz, seq, D_MODEL)
```
